```python
import jax, jax.numpy as jnp
from jax import lax
import numpy as np

D_MODEL = 2048
BATCH = 2
SEQ = 4096
DEPTH = 2
DEC_BATCH = 128
DEC_SEQ = 4
PAST_LEN = 8192
PAGE_SIZE = 128

N_META = 16
POOL_WIDTH = D_MODEL // 2
POOL_WINDOWS = (2, 4, 8, 16)
POOL_GROUPS = len(POOL_WINDOWS)
POOL_GROUP_DIM = POOL_WIDTH // POOL_GROUPS
POOL_STATE = max(POOL_WINDOWS) - 1
N_HEADS = D_MODEL // 256
NOPE_DIM = 128
ROPE_DIM = 64
ROPE_HALF = ROPE_DIM // 2
QK_DIM = NOPE_DIM + ROPE_DIM
V_DIM = 128
ATTN_WIDTH = N_HEADS * V_DIM
MIX_WIDTH = POOL_WIDTH + ATTN_WIDTH
Q_LORA = D_MODEL // 4
KV_LORA = D_MODEL // 8
IN_WIDTH = POOL_WIDTH + Q_LORA + KV_LORA + ROPE_DIM
D_FF = 5632
ROPE_THETA = 10000.0
EPS = 1e-6
Q_BLOCK = 128
ATTN_SCALE = QK_DIM ** -0.5
NEG_INF = -1e30
F32 = jnp.float32

kernel_name = "hymba_pool_mla_macaron_step"


def rmsnorm(x, g):
    xf = x.astype(F32)
    xf = xf * lax.rsqrt(jnp.mean(xf * xf, axis=-1, keepdims=True) + EPS)
    return (xf * g.astype(F32)).astype(x.dtype)


def swiglu_half(x, g, w_gate, w_up, w_down):
    h = rmsnorm(x, g)
    a = jax.nn.silu(h @ w_gate) * (h @ w_up)
    return x + 0.5 * (a @ w_down)


def rope_tables(pos):
    inv = ROPE_THETA ** (-jnp.arange(ROPE_HALF, dtype=F32) / ROPE_HALF)
    ang = pos.astype(F32)[:, None] * inv[None, :]
    return jnp.cos(ang), jnp.sin(ang)


def apply_rope(x, cos, sin):
    shape = (x.shape[1],) + (1,) * (x.ndim - 3) + (ROPE_HALF,)
    c, s = cos.reshape(shape), sin.reshape(shape)
    x1 = x[..., :ROPE_HALF].astype(F32)
    x2 = x[..., ROPE_HALF:].astype(F32)
    return jnp.concatenate([x1 * c - x2 * s, x1 * s + x2 * c], -1).astype(x.dtype)


def qk_norm(v, g_nope, g_rope):
    return rmsnorm(v, jnp.concatenate([g_nope, g_rope, g_rope], -1))


def mixer_inputs(x, pos, mix_norm, w_in, q_a_norm, w_q_b, kv_a_norm, q_norm_nope, q_norm_rope):
    h = rmsnorm(x, mix_norm)
    z = h @ w_in
    o1, o2, o3 = POOL_WIDTH, POOL_WIDTH + Q_LORA, POOL_WIDTH + Q_LORA + KV_LORA
    u, q_lat, kv_lat, kpe_raw = z[..., :o1], z[..., o1:o2], z[..., o2:o3], z[..., o3:]
    cos, sin = rope_tables(pos)
    q = (rmsnorm(q_lat, q_a_norm) @ w_q_b).reshape(x.shape[:2] + (N_HEADS, QK_DIM))
    q = jnp.concatenate([q[..., :NOPE_DIM], apply_rope(q[..., NOPE_DIM:], cos, sin)], -1)
    q = qk_norm(q, q_norm_nope, q_norm_rope)
    c_kv = rmsnorm(kv_lat, kv_a_norm)
    k_pe = apply_rope(kpe_raw, cos, sin)
    return u, q, c_kv, k_pe


def expand_kv(c_kv, k_pe, w_kv_b, k_norm_nope, k_norm_rope):
    kv = jnp.einsum('...c,chd->...hd', c_kv, w_kv_b.reshape(KV_LORA, N_HEADS, NOPE_DIM + V_DIM))
    k_nope, v = kv[..., :NOPE_DIM], kv[..., NOPE_DIM:]
    k_pe_h = jnp.broadcast_to(k_pe[..., None, :], k_nope.shape[:-1] + (ROPE_DIM,))
    k = qk_norm(jnp.concatenate([k_nope, k_pe_h], -1), k_norm_nope, k_norm_rope)
    return k, v


def pool_mix(u_ext, n_prefix, pool_w, pool_scale):
    L = u_ext.shape[1]
    cs = jnp.cumsum(u_ext.astype(F32), axis=1)
    count = jnp.arange(1, L + 1, dtype=F32)[None, :, None]
    outs = []
    for g, w in enumerate(POOL_WINDOWS):
        sl = slice(g * POOL_GROUP_DIM, (g + 1) * POOL_GROUP_DIM)
        c = cs[..., sl]
        shifted = jnp.pad(c, ((0, 0), (w, 0), (0, 0)))[:, :L]
        d = (c - shifted) / jnp.minimum(count, float(w)) - u_ext[..., sl].astype(F32)
        d = d[:, n_prefix:].astype(u_ext.dtype)
        outs.append(jnp.einsum('btc,cd->btd', d, pool_w[g]))
    return jnp.concatenate(outs, -1) * pool_scale


def prompt_attention(q, k, v):
    B, T = q.shape[0], q.shape[1]
    n_blk = -(-T // Q_BLOCK)
    Tp = n_blk * Q_BLOCK
    qb = jnp.pad(q, ((0, 0), (0, Tp - T), (0, 0), (0, 0)))
    qb = qb.reshape(B, n_blk, Q_BLOCK, N_HEADS, QK_DIM).transpose(1, 0, 2, 3, 4)
    kpos = jnp.arange(T)

    def block(args):
        i, qi = args
        s = jnp.einsum('bqhd,bkhd->bhqk', qi, k, preferred_element_type=F32) * ATTN_SCALE
        qpos = i * Q_BLOCK + jnp.arange(Q_BLOCK)
        s = jnp.where(kpos[None, :] <= qpos[:, None], s, NEG_INF)
        p = jax.nn.softmax(s, axis=-1).astype(v.dtype)
        return jnp.einsum('bhqk,bkhd->bqhd', p, v)

    out = lax.map(block, (jnp.arange(n_blk), qb))
    return out.transpose(1, 0, 2, 3, 4).reshape(B, Tp, N_HEADS, V_DIM)[:, :T]


def sample_attention(q, c_new, kpe_new, cache_ckv, cache_kpe, layer, page_table,
                     w_kv_b, k_norm_nope, k_norm_rope):
    n_past = page_table.shape[1] * PAGE_SIZE
    S = q.shape[1]

    def one(args):
        qi, ci, pi, pages = args
        past_c = cache_ckv[layer, pages].reshape(n_past, KV_LORA).astype(ci.dtype)
        past_p = cache_kpe[layer, pages].reshape(n_past, ROPE_DIM).astype(pi.dtype)
        c_all = jnp.concatenate([past_c, ci], 0)
        p_all = jnp.concatenate([past_p, pi], 0)
        k, v = expand_kv(c_all, p_all, w_kv_b, k_norm_nope, k_norm_rope)
        s = jnp.einsum('shd,khd->hsk', qi, k, preferred_element_type=F32) * ATTN_SCALE
        kpos = jnp.arange(n_past + S)
        qpos = n_past + jnp.arange(S)
        s = jnp.where(kpos[None, :] <= qpos[:, None], s, NEG_INF)
        p = jax.nn.softmax(s, axis=-1).astype(v.dtype)
        return jnp.einsum('hsk,khd->shd', p, v)

    return lax.map(one, (q, c_new, kpe_new, page_table))


def merge_groups(x, pool_out, attn_out, pool_out_norm, attn_out_norm, w_out):
    a = attn_out.reshape(attn_out.shape[:2] + (ATTN_WIDTH,))
    cat = jnp.concatenate([rmsnorm(pool_out, pool_out_norm), rmsnorm(a, attn_out_norm)], -1)
    return x + cat @ w_out


def setup_inputs(seed: int = 0) -> dict:
    key = jax.random.key(seed)
    ks = iter(jax.random.split(key, 48))

    def nrm(shape, scale):
        return jax.random.normal(next(ks), shape, F32) * scale

    def gain(shape):
        return 1.0 + 0.05 * jax.random.normal(next(ks), shape, F32)

    n_pages = PAST_LEN // PAGE_SIZE
    n_used = DEC_BATCH * n_pages
    n_pool = n_used + max(n_used // 4, 1)
    page_table = jax.random.permutation(next(ks), n_pool)[:n_used].reshape(DEC_BATCH, n_pages).astype(jnp.int32)
    return {
        'x_prompt': nrm((BATCH, SEQ, D_MODEL), 1.0),
        'x_sample': nrm((DEC_BATCH, DEC_SEQ, D_MODEL), 1.0),
        'cache_ckv': nrm((DEPTH, n_pool, PAGE_SIZE, KV_LORA), 1.0),
        'cache_kpe': nrm((DEPTH, n_pool, PAGE_SIZE, ROPE_DIM), 1.0),
        'state_pool': nrm((DEPTH, DEC_BATCH, POOL_STATE, POOL_WIDTH), 1.0),
        'page_table': page_table,
        'meta_tokens': nrm((N_META, D_MODEL), 1.0),
        'ffn1_norm': gain((DEPTH, D_MODEL)),
        'ffn1_w_gate': nrm((DEPTH, D_MODEL, D_FF), D_MODEL ** -0.5),
        'ffn1_w_up': nrm((DEPTH, D_MODEL, D_FF), D_MODEL ** -0.5),
        'ffn1_w_down': nrm((DEPTH, D_FF, D_MODEL), D_FF ** -0.5),
        'mix_norm': gain((DEPTH, D_MODEL)),
        'w_in': nrm((DEPTH, D_MODEL, IN_WIDTH), D_MODEL ** -0.5),
        'pool_w': nrm((DEPTH, POOL_GROUPS, POOL_GROUP_DIM, POOL_GROUP_DIM), POOL_GROUP_DIM ** -0.5),
        'pool_scale': gain((DEPTH, POOL_WIDTH)),
        'q_a_norm': gain((DEPTH, Q_LORA)),
        'w_q_b': nrm((DEPTH, Q_LORA, N_HEADS * QK_DIM), Q_LORA ** -0.5),
        'kv_a_norm': gain((DEPTH, KV_LORA)),
        'w_kv_b': nrm((DEPTH, KV_LORA, N_HEADS * (NOPE_DIM + V_DIM)), KV_LORA ** -0.5),
        'q_norm_nope': gain((DEPTH, NOPE_DIM)),
        'q_norm_rope': gain((DEPTH, ROPE_HALF)),
        'k_norm_nope': gain((DEPTH, NOPE_DIM)),
        'k_norm_rope': gain((DEPTH, ROPE_HALF)),
        'pool_out_norm': gain((DEPTH, POOL_WIDTH)),
        'attn_out_norm': gain((DEPTH, ATTN_WIDTH)),
        'w_out': nrm((DEPTH, MIX_WIDTH, D_MODEL), MIX_WIDTH ** -0.5),
        'ffn2_norm': gain((DEPTH, D_MODEL)),
        'ffn2_w_gate': nrm((DEPTH, D_MODEL, D_FF), D_MODEL ** -0.5),
        'ffn2_w_up': nrm((DEPTH, D_MODEL, D_FF), D_MODEL ** -0.5),
        'ffn2_w_down': nrm((DEPTH, D_FF, D_MODEL), D_FF ** -0.5),
    }


def reference(x_prompt, x_sample, cache_ckv, cache_kpe, state_pool, page_table,
              meta_tokens, ffn1_norm, ffn1_w_gate, ffn1_w_up, ffn1_w_down,
              mix_norm, w_in, pool_w, pool_scale, q_a_norm, w_q_b, kv_a_norm, w_kv_b,
              q_norm_nope, q_norm_rope, k_norm_nope, k_norm_rope,
              pool_out_norm, attn_out_norm, w_out,
              ffn2_norm, ffn2_w_gate, ffn2_w_up, ffn2_w_down):
    B = x_prompt.shape[0]
    T = N_META + x_prompt.shape[1]
    meta = jnp.broadcast_to(meta_tokens[None].astype(x_prompt.dtype), (B, N_META, D_MODEL))
    xp = jnp.concatenate([meta, x_prompt], axis=1)
    xs = x_sample
    n_past = page_table.shape[1] * PAGE_SIZE
    pos_p = jnp.arange(T)
    pos_s = n_past + jnp.arange(x_sample.shape[1])

    ckv_p, kpe_p, pool_p, ckv_s, kpe_s, pool_s = [], [], [], [], [], []
    for l in range(DEPTH):
        xp = swiglu_half(xp, ffn1_norm[l], ffn1_w_gate[l], ffn1_w_up[l], ffn1_w_down[l])
        xs = swiglu_half(xs, ffn1_norm[l], ffn1_w_gate[l], ffn1_w_up[l], ffn1_w_down[l])

        u, q, c, kpe = mixer_inputs(xp, pos_p, mix_norm[l], w_in[l], q_a_norm[l], w_q_b[l],
                                    kv_a_norm[l], q_norm_nope[l], q_norm_rope[l])
        k, v = expand_kv(c, kpe, w_kv_b[l], k_norm_nope[l], k_norm_rope[l])
        attn = prompt_attention(q, k, v)
        pool = pool_mix(u, 0, pool_w[l], pool_scale[l])
        xp = merge_groups(xp, pool, attn, pool_out_norm[l], attn_out_norm[l], w_out[l])
        ckv_p.append(c)
        kpe_p.append(kpe)
        pool_p.append(u[:, -POOL_STATE:])

        us, qs, cs, kpes = mixer_inputs(xs, pos_s, mix_norm[l], w_in[l], q_a_norm[l], w_q_b[l],
                                        kv_a_norm[l], q_norm_nope[l], q_norm_rope[l])
        attn_s = sample_attention(qs, cs, kpes, cache_ckv, cache_kpe, l, page_table,
                                  w_kv_b[l], k_norm_nope[l], k_norm_rope[l])
        u_ext = jnp.concatenate([state_pool[l].astype(us.dtype), us], axis=1)
        pool_sm = pool_mix(u_ext, POOL_STATE, pool_w[l], pool_scale[l])
        xs = merge_groups(xs, pool_sm, attn_s, pool_out_norm[l], attn_out_norm[l], w_out[l])
        ckv_s.append(cs)
        kpe_s.append(kpes)
        pool_s.append(u_ext[:, -POOL_STATE:])

        xp = swiglu_half(xp, ffn2_norm[l], ffn2_w_gate[l], ffn2_w_up[l], ffn2_w_down[l])
        xs = swiglu_half(xs, ffn2_norm[l], ffn2_w_gate[l], ffn2_w_up[l], ffn2_w_down[l])

    y_prompt = xp[:, N_META:]
    y_sample = xs
    return (y_prompt, y_sample,
            jnp.stack(ckv_p), jnp.stack(kpe_p), jnp.stack(pool_p),
            jnp.stack(ckv_s), jnp.stack(kpe_s), jnp.stack(pool_s))
```

```python
import functools

import jax
import jax.numpy as jnp
from jax import lax
from jax.experimental import pallas as pl
from jax.experimental.pallas import tpu as pltpu

F32 = jnp.float32
BF16 = jnp.bfloat16

D_MODEL = 2048
N_META = 16
POOL_WIDTH = 1024
POOL_WINDOWS = (2, 4, 8, 16)
POOL_GROUP_DIM = 256
POOL_STATE = 15
N_HEADS = 8
NOPE_DIM = 128
ROPE_DIM = 64
ROPE_HALF = 32
QK_DIM = 192
QK_PAD = 256
V_DIM = 128
ATTN_WIDTH = 1024
Q_LORA = 512
KV_LORA = 256
D_FF = 5632
ROPE_THETA = 10000.0
EPS = 1e-6
ATTN_SCALE = QK_DIM ** -0.5
NEG_INF = -1e30
PAGE_SIZE = 128

IN_EXT = POOL_WIDTH + Q_LORA + KV_LORA + 2 * ROPE_DIM
FF_TILE = 512
ATT_BLOCK = 256
HALO = 16
KEY_CHUNK = 256
VMEM_LIMIT = 56 * 1024 * 1024


def _rms(x, g):
    return x * lax.rsqrt(jnp.mean(x * x, axis=-1, keepdims=True) + EPS) * g


def _dot(a, b):
    return jnp.dot(a, b, preferred_element_type=F32)


def _dot_nt(a, b):
    return lax.dot_general(a, b, (((1,), (1,)), ((), ())), preferred_element_type=F32)


def _params(sem, vmem=VMEM_LIMIT):
    return pltpu.CompilerParams(dimension_semantics=sem, vmem_limit_bytes=vmem)


def _ffn_kernel(x_ref, g_ref, wg_ref, wu_ref, wd_ref, o_ref, h_ref):
    j = pl.program_id(1)

    @pl.when(j == 0)
    def _():
        x = x_ref[...]
        h_ref[...] = _rms(x, g_ref[...]).astype(BF16)
        o_ref[...] = x

    h = h_ref[...]
    gate = _dot(h, wg_ref[...])
    up = _dot(h, wu_ref[...])
    a = (0.5 * gate * jax.nn.sigmoid(gate) * up).astype(BF16)
    o_ref[...] += _dot(a, wd_ref[...])


def _ffn(x, g, wg, wu, wd, *, tm):
    n = x.shape[0]
    return pl.pallas_call(
        _ffn_kernel,
        out_shape=jax.ShapeDtypeStruct((n, D_MODEL), F32),
        grid=(n // tm, D_FF // FF_TILE),
        in_specs=[
            pl.BlockSpec((tm, D_MODEL), lambda i, j: (i, 0)),
            pl.BlockSpec((1, D_MODEL), lambda i, j: (0, 0)),
            pl.BlockSpec((D_MODEL, FF_TILE), lambda i, j: (0, j)),
            pl.BlockSpec((D_MODEL, FF_TILE), lambda i, j: (0, j)),
            pl.BlockSpec((FF_TILE, D_MODEL), lambda i, j: (j, 0)),
        ],
        out_specs=pl.BlockSpec((tm, D_MODEL), lambda i, j: (i, 0)),
        scratch_shapes=[pltpu.VMEM((tm, D_MODEL), BF16)],
        compiler_params=_params(("parallel", "arbitrary")),
        name="ffn",
    )(x, g, wg, wu, wd)


def _inproj_kernel(x_ref, gmix_ref, win_ref, gqa_ref, wq_ref, gkva_ref, wk_ref, wv_ref,
                   cs_ref, gq_ref, gk_ref,
                   u_ref, ckv_ref, kpe_ref, q_ref, k_ref, v_ref):
    h = _rms(x_ref[...], gmix_ref[...]).astype(BF16)
    z = _dot(h, win_ref[...])
    u_ref[...] = z[:, :POOL_WIDTH]
    ql = _rms(z[:, POOL_WIDTH:POOL_WIDTH + Q_LORA], gqa_ref[...]).astype(BF16)
    o_kv = POOL_WIDTH + Q_LORA
    c = _rms(z[:, o_kv:o_kv + KV_LORA], gkva_ref[...])
    ckv_ref[...] = c

    cs = cs_ref[...]
    lane = lax.broadcasted_iota(jnp.int32, cs.shape, 1)
    low = lane < ROPE_DIM

    def rope(t):
        t = t * cs
        return jnp.where(low, t + pltpu.roll(t, ROPE_DIM, axis=1), 0.0)

    kpe = rope(z[:, o_kv + KV_LORA:])
    kpe_ref[...] = kpe
    kpe_ss = jnp.sum(kpe * kpe, axis=-1, keepdims=True)

    q = _dot(ql, wq_ref[...])
    cb = c.astype(BF16)
    kn = _dot(cb, wk_ref[...])
    v = _dot(cb, wv_ref[...])
    gq = gq_ref[...]
    gk = gk_ref[...]
    inv_d = 1.0 / QK_DIM
    for hd in range(N_HEADS):
        qa = q[:, hd * QK_PAD:hd * QK_PAD + NOPE_DIM]
        qb = rope(q[:, hd * QK_PAD + NOPE_DIM:(hd + 1) * QK_PAD])
        rs = lax.rsqrt(jnp.sum(qa * qa + qb * qb, axis=-1, keepdims=True) * inv_d + EPS)
        q_ref[hd, :, :NOPE_DIM] = (qa * rs * gq[:, :NOPE_DIM]).astype(BF16)
        q_ref[hd, :, NOPE_DIM:] = (qb * rs * gq[:, NOPE_DIM:]).astype(BF16)
        ka = kn[:, hd * NOPE_DIM:(hd + 1) * NOPE_DIM]
        rk = lax.rsqrt((jnp.sum(ka * ka, axis=-1, keepdims=True) + kpe_ss) * inv_d + EPS)
        k_ref[hd, :, :NOPE_DIM] = (ka * rk * gk[:, :NOPE_DIM]).astype(BF16)
        k_ref[hd, :, NOPE_DIM:] = (kpe * rk * gk[:, NOPE_DIM:]).astype(BF16)
        v_ref[hd] = v[:, hd * V_DIM:(hd + 1) * V_DIM].astype(BF16)


def _inproj(x, cs, w, *, tm):
    n = x.shape[0]
    row = lambda i: (i, 0)
    fix = lambda i: (0, 0)
    head = lambda i: (0, i, 0)
    return pl.pallas_call(
        _inproj_kernel,
        out_shape=(
            jax.ShapeDtypeStruct((n, POOL_WIDTH), F32),
            jax.ShapeDtypeStruct((n, KV_LORA), F32),
            jax.ShapeDtypeStruct((n, 2 * ROPE_DIM), F32),
            jax.ShapeDtypeStruct((N_HEADS, n, QK_PAD), BF16),
            jax.ShapeDtypeStruct((N_HEADS, n, QK_PAD), BF16),
            jax.ShapeDtypeStruct((N_HEADS, n, V_DIM), BF16),
        ),
        grid=(n // tm,),
        in_specs=[
            pl.BlockSpec((tm, D_MODEL), row),
            pl.BlockSpec((1, D_MODEL), fix),
            pl.BlockSpec((D_MODEL, IN_EXT), fix),
            pl.BlockSpec((1, Q_LORA), fix),
            pl.BlockSpec((Q_LORA, N_HEADS * QK_PAD), fix),
            pl.BlockSpec((1, KV_LORA), fix),
            pl.BlockSpec((KV_LORA, N_HEADS * NOPE_DIM), fix),
            pl.BlockSpec((KV_LORA, N_HEADS * V_DIM), fix),
            pl.BlockSpec((tm, 2 * ROPE_DIM), row),
            pl.BlockSpec((1, QK_PAD), fix),
            pl.BlockSpec((1, QK_PAD), fix),
        ],
        out_specs=(
            pl.BlockSpec((tm, POOL_WIDTH), row),
            pl.BlockSpec((tm, KV_LORA), row),
            pl.BlockSpec((tm, 2 * ROPE_DIM), row),
            pl.BlockSpec((N_HEADS, tm, QK_PAD), head),
            pl.BlockSpec((N_HEADS, tm, QK_PAD), head),
            pl.BlockSpec((N_HEADS, tm, V_DIM), head),
        ),
        compiler_params=_params(("parallel",)),
        name="inproj",
    )(x, w["mix_norm"], w["w_in"], w["q_a_norm"], w["wq"], w["kv_a_norm"], w["wk"], w["wv"],
      cs, w["gq"], w["gk"])


def _flash_kernel(q_ref, k_ref, v_ref, o_ref):
    i = pl.program_id(2)
    q = q_ref[...]
    bq = q.shape[0]

    def step(kb, vb, carry, mask):
        m, l, acc = carry
        s = _dot_nt(q, kb)
        if mask is not None:
            s = jnp.where(mask, s, NEG_INF)
        m_new = jnp.maximum(m, jnp.max(s, axis=-1, keepdims=True))
        p = jnp.exp(s - m_new)
        alpha = jnp.exp(m - m_new)
        l = alpha * l + jnp.sum(p, axis=-1, keepdims=True)
        acc = alpha * acc + _dot(p.astype(BF16), vb)
        return m_new, l, acc

    def body(j, carry):
        off = pl.multiple_of(j * ATT_BLOCK, ATT_BLOCK)
        return step(k_ref[pl.ds(off, ATT_BLOCK), :], v_ref[pl.ds(off, ATT_BLOCK), :], carry, None)

    carry = (jnp.full((bq, 1), -jnp.inf, F32), jnp.zeros((bq, 1), F32), jnp.zeros((bq, V_DIM), F32))
    carry = lax.fori_loop(0, i, body, carry)
    off = pl.multiple_of(i * ATT_BLOCK, ATT_BLOCK)
    row = lax.broadcasted_iota(jnp.int32, (bq, ATT_BLOCK), 0)
    col = lax.broadcasted_iota(jnp.int32, (bq, ATT_BLOCK), 1)
    _, l, acc = step(k_ref[pl.ds(off, ATT_BLOCK), :], v_ref[pl.ds(off, ATT_BLOCK), :], carry, col <= row)
    o_ref[...] = acc / l


def _flash(q, k, v, *, batch, t_pad):
    n = q.shape[1]
    nq = t_pad // ATT_BLOCK
    qmap = lambda b, h, i: (h, b * nq + i, 0)
    kvmap = lambda b, h, i: (h, b, 0)
    return pl.pallas_call(
        _flash_kernel,
        out_shape=jax.ShapeDtypeStruct((N_HEADS, n, V_DIM), F32),
        grid=(batch, N_HEADS, nq),
        in_specs=[
            pl.BlockSpec((None, ATT_BLOCK, QK_PAD), qmap),
            pl.BlockSpec((None, t_pad, QK_PAD), kvmap),
            pl.BlockSpec((None, t_pad, V_DIM), kvmap),
        ],
        out_specs=pl.BlockSpec((None, ATT_BLOCK, V_DIM), qmap),
        compiler_params=_params(("parallel", "parallel", "arbitrary")),
        name="flash",
    )(q, k, v)


def _merge_tail(x, d, a, pw_ref, ps_ref, pn_ref, an_ref, wo_ref):
    pool = jnp.concatenate(
        [_dot(d[:, g * POOL_GROUP_DIM:(g + 1) * POOL_GROUP_DIM].astype(BF16), pw_ref[g])
         for g in range(len(POOL_WINDOWS))], axis=-1) * ps_ref[...]
    cat = jnp.concatenate([_rms(pool, pn_ref[...]).astype(BF16),
                           _rms(a, an_ref[...]).astype(BF16)], axis=-1)
    return x + _dot(cat, wo_ref[...])


def _merge_prompt_kernel(x_ref, u_ref, halo_ref, a_ref, pw_ref, ps_ref, pn_ref, an_ref, wo_ref,
                         o_ref, ext_ref):
    i = pl.program_id(1)
    tm = u_ref.shape[0]
    ext_ref[0:HALO, :] = jnp.where(i > 0, halo_ref[...], 0.0)
    ext_ref[HALO:HALO + tm, :] = u_ref[...]
    pos = i * tm + lax.broadcasted_iota(jnp.int32, (tm, 1), 0)
    ds = []
    for g, w in enumerate(POOL_WINDOWS):
        sl = slice(g * POOL_GROUP_DIM, (g + 1) * POOL_GROUP_DIM)
        tok = ext_ref[HALO:HALO + tm, sl]
        acc = tok
        for k in range(1, w):
            acc = acc + ext_ref[HALO - k:HALO - k + tm, sl]
        cnt = jnp.minimum(pos + 1, w).astype(F32)
        ds.append(acc / cnt - tok)
    d = jnp.concatenate(ds, axis=-1)
    a = jnp.concatenate([a_ref[hd] for hd in range(N_HEADS)], axis=-1)
    o_ref[...] = _merge_tail(x_ref[...], d, a, pw_ref, ps_ref, pn_ref, an_ref, wo_ref)


def _merge_weight_specs(fix2, fix3):
    return [
        pl.BlockSpec((len(POOL_WINDOWS), POOL_GROUP_DIM, POOL_GROUP_DIM), fix3),
        pl.BlockSpec((1, POOL_WIDTH), fix2),
        pl.BlockSpec((1, POOL_WIDTH), fix2),
        pl.BlockSpec((1, ATTN_WIDTH), fix2),
        pl.BlockSpec((D_MODEL, D_MODEL), fix2),
    ]


def _merge_prompt(x, u, attn, w, *, batch, t_pad, tm):
    n = x.shape[0]
    nt = t_pad // tm
    row = lambda b, i: (b * nt + i, 0)
    return pl.pallas_call(
        _merge_prompt_kernel,
        out_shape=jax.ShapeDtypeStruct((n, D_MODEL), F32),
        grid=(batch, nt),
        in_specs=[
            pl.BlockSpec((tm, D_MODEL), row),
            pl.BlockSpec((tm, POOL_WIDTH), row),
            pl.BlockSpec((HALO, POOL_WIDTH),
                         lambda b, i: (jnp.maximum((b * nt + i) * (tm // HALO) - 1, 0), 0)),
            pl.BlockSpec((N_HEADS, tm, V_DIM), lambda b, i: (0, b * nt + i, 0)),
        ] + _merge_weight_specs(lambda b, i: (0, 0), lambda b, i: (0, 0, 0)),
        out_specs=pl.BlockSpec((tm, D_MODEL), row),
        scratch_shapes=[pltpu.VMEM((HALO + tm, POOL_WIDTH), F32)],
        compiler_params=_params(("parallel", "arbitrary")),
        name="merge_prompt",
    )(x, u, u, attn, w["pool_w"], w["pool_scale"], w["pool_out_norm"], w["attn_out_norm"], w["w_out"])


def _merge_sample_kernel(x_ref, ue_ref, pc_ref, wv_ref, pw_ref, ps_ref, pn_ref, an_ref, wo_ref, o_ref):
    s = pl.program_id(0)
    ds = []
    for g, w in enumerate(POOL_WINDOWS):
        sl = slice(g * POOL_GROUP_DIM, (g + 1) * POOL_GROUP_DIM)
        tok = ue_ref[POOL_STATE + s, :, sl]
        acc = tok
        for k in range(1, w):
            acc = acc + ue_ref[POOL_STATE + s - k, :, sl]
        ds.append(acc * (1.0 / w) - tok)
    d = jnp.concatenate(ds, axis=-1)
    a = jnp.concatenate([_dot(pc_ref[hd].astype(BF16), wv_ref[hd]) for hd in range(N_HEADS)], axis=-1)
    o_ref[...] = _merge_tail(x_ref[...], d, a, pw_ref, ps_ref, pn_ref, an_ref, wo_ref)


def _merge_sample(x, ue, pc, w, *, dec_seq, dec_batch):
    n = x.shape[0]
    n_ext = ue.shape[0]
    return pl.pallas_call(
        _merge_sample_kernel,
        out_shape=jax.ShapeDtypeStruct((n, D_MODEL), F32),
        grid=(dec_seq,),
        in_specs=[
            pl.BlockSpec((dec_batch, D_MODEL), lambda s: (s, 0)),
            pl.BlockSpec((n_ext, dec_batch, POOL_WIDTH), lambda s: (0, 0, 0)),
            pl.BlockSpec((N_HEADS, dec_batch, KV_LORA), lambda s: (0, s, 0)),
            pl.BlockSpec((N_HEADS, KV_LORA, V_DIM), lambda s: (0, 0, 0)),
        ] + _merge_weight_specs(lambda s: (0, 0), lambda s: (0, 0, 0)),
        out_specs=pl.BlockSpec((dec_batch, D_MODEL), lambda s: (s, 0)),
        compiler_params=_params(("arbitrary",)),
        name="merge_sample",
    )(x, ue, pc, w["wv3"], w["pool_w"], w["pool_scale"], w["pool_out_norm"], w["attn_out_norm"], w["w_out"])


def _qabs_kernel(q_ref, wk_ref, gk_ref, qa_ref, qr_ref):
    q = q_ref[...].astype(F32)
    gk = gk_ref[...]
    qn = (q[:, :NOPE_DIM] * gk[:, :NOPE_DIM]).astype(BF16)
    qa_ref[...] = _dot_nt(qn, wk_ref[...]).astype(BF16)
    qr_ref[...] = (q[:, NOPE_DIM:] * gk[:, NOPE_DIM:]).astype(BF16)


def _qabs(q, w):
    n = q.shape[1]
    return pl.pallas_call(
        _qabs_kernel,
        out_shape=(jax.ShapeDtypeStruct((N_HEADS, n, KV_LORA), BF16),
                   jax.ShapeDtypeStruct((N_HEADS, n, QK_PAD - NOPE_DIM), BF16)),
        grid=(N_HEADS,),
        in_specs=[
            pl.BlockSpec((None, n, QK_PAD), lambda h: (h, 0, 0)),
            pl.BlockSpec((None, KV_LORA, NOPE_DIM), lambda h: (h, 0, 0)),
            pl.BlockSpec((1, QK_PAD), lambda h: (0, 0)),
        ],
        out_specs=(pl.BlockSpec((None, n, KV_LORA), lambda h: (h, 0, 0)),
                   pl.BlockSpec((None, n, QK_PAD - NOPE_DIM), lambda h: (h, 0, 0))),
        compiler_params=_params(("parallel",)),
        name="qabs",
    )(q, w["wk3"], w["gk"])


def _sattn_kernel(pt_ref, qa_ref, qr_ref, cnew_ref, pnew_ref, wkt_ref, ckv_hbm, kpe_hbm, o_ref,
                  a_ref, cbuf, pbuf, sem, *, layer, n_pages, n_rows):
    b = pl.program_id(0)
    nb = pl.num_programs(0)
    slot = b % 2
    n_kn = N_HEADS * NOPE_DIM

    def page_copies(page, p, sl):
        rows = pl.ds(pl.multiple_of(p * PAGE_SIZE, PAGE_SIZE), PAGE_SIZE)
        return (pltpu.make_async_copy(ckv_hbm.at[layer, page], cbuf.at[sl, rows], sem.at[0, sl]),
                pltpu.make_async_copy(kpe_hbm.at[layer, page], pbuf.at[sl, rows], sem.at[1, sl]))

    def issue(seq, sl):
        def body(p, carry):
            for cp in page_copies(pt_ref[seq, p], p, sl):
                cp.start()
            return carry
        lax.fori_loop(0, n_pages, body, 0)

    def wait(sl):
        def body(p, carry):
            for cp in page_copies(0, p, sl):
                cp.wait()
            return carry
        lax.fori_loop(0, n_pages, body, 0)

    @pl.when(b == 0)
    def _():
        a_ref[0:n_kn, :] = wkt_ref[...]
        issue(0, 0)

    @pl.when(b + 1 < nb)
    def _():
        issue(b + 1, 1 - slot)

    a_ref[n_kn:n_kn + n_rows, :] = qa_ref[...]
    qr = qr_ref[...]
    ones = jnp.ones((16, ROPE_DIM), BF16)

    def chunk(cb, pf, carry, mask):
        m, l, acc = carry
        nk = cb.shape[0]
        r_all = _dot_nt(a_ref[...], cb)
        kn = r_all[:n_kn]
        ssq = jnp.sum((kn * kn).reshape(N_HEADS, NOPE_DIM, nk), axis=1)
        raw_r = _dot_nt(qr, pf.astype(BF16))
        pe2 = _dot_nt(ones, (pf * pf).astype(BF16))[:N_HEADS]
        r = lax.rsqrt((ssq + pe2) * (1.0 / QK_DIM) + EPS)
        s = (r_all[n_kn:] + raw_r) * jnp.concatenate([r] * (n_rows // N_HEADS), axis=0)
        if mask is not None:
            s = jnp.where(mask, s, NEG_INF)
        m_new = jnp.maximum(m, jnp.max(s, axis=-1, keepdims=True))
        p = jnp.exp(s - m_new)
        alpha = jnp.exp(m - m_new)
        l = alpha * l + jnp.sum(p, axis=-1, keepdims=True)
        acc = alpha * acc + _dot(p.astype(BF16), cb)
        return m_new, l, acc

    wait(slot)

    def body(c, carry):
        rows = pl.ds(pl.multiple_of(c * KEY_CHUNK, KEY_CHUNK), KEY_CHUNK)
        return chunk(cbuf[slot, rows, :].astype(BF16), pbuf[slot, rows, :], carry, None)

    carry = (jnp.full((n_rows, 1), -jnp.inf, F32), jnp.zeros((n_rows, 1), F32),
             jnp.zeros((n_rows, KV_LORA), F32))
    carry = lax.fori_loop(0, n_pages * PAGE_SIZE // KEY_CHUNK, body, carry)
    qpos = lax.broadcasted_iota(jnp.int32, (n_rows, PAGE_SIZE), 0) // N_HEADS
    kidx = lax.broadcasted_iota(jnp.int32, (n_rows, PAGE_SIZE), 1)
    _, l, acc = chunk(cnew_ref[...], pnew_ref[...], carry, kidx <= qpos)
    o_ref[...] = acc / l


def _sattn(page_table, qa, qr, cnew, pnew, wkt, cache_ckv, cache_kpe, *, layer):
    nb, n_pages = page_table.shape
    n_rows = qa.shape[1]
    n_keys = n_pages * PAGE_SIZE
    seq3 = lambda b, pt: (b, 0, 0)
    grid_spec = pltpu.PrefetchScalarGridSpec(
        num_scalar_prefetch=1,
        grid=(nb,),
        in_specs=[
            pl.BlockSpec((None, n_rows, KV_LORA), seq3),
            pl.BlockSpec((None, n_rows, ROPE_DIM), seq3),
            pl.BlockSpec((None, PAGE_SIZE, KV_LORA), seq3),
            pl.BlockSpec((None, PAGE_SIZE, ROPE_DIM), seq3),
            pl.BlockSpec((N_HEADS * NOPE_DIM, KV_LORA), lambda b, pt: (0, 0)),
            pl.BlockSpec(memory_space=pl.ANY),
            pl.BlockSpec(memory_space=pl.ANY),
        ],
        out_specs=pl.BlockSpec((None, n_rows, KV_LORA), seq3),
        scratch_shapes=[
            pltpu.VMEM((N_HEADS * NOPE_DIM + n_rows, KV_LORA), BF16),
            pltpu.VMEM((2, n_keys, KV_LORA), F32),
            pltpu.VMEM((2, n_keys, ROPE_DIM), F32),
            pltpu.SemaphoreType.DMA((2, 2)),
        ],
    )
    return pl.pallas_call(
        functools.partial(_sattn_kernel, layer=layer, n_pages=n_pages, n_rows=n_rows),
        out_shape=jax.ShapeDtypeStruct((nb, n_rows, KV_LORA), F32),
        grid_spec=grid_spec,
        compiler_params=_params(("arbitrary",)),
        name="sattn",
    )(page_table, qa, qr, cnew, pnew, wkt, cache_ckv, cache_kpe)


def _rot_half_cols(w):
    return jnp.concatenate([-w[..., ROPE_HALF:], w[..., :ROPE_HALF]], axis=-1)


def _rope_table(pos):
    inv = ROPE_THETA ** (-jnp.arange(ROPE_HALF, dtype=F32) / ROPE_HALF)
    ang = pos.astype(F32)[:, None] * inv[None, :]
    c, s = jnp.cos(ang), jnp.sin(ang)
    return jnp.concatenate([c, c, s, s], axis=-1)


def _layer_weights(l, p):
    row = lambda v: v[l][None, :].astype(F32)
    w_in = p["w_in"][l]
    w_pe = w_in[:, POOL_WIDTH + Q_LORA + KV_LORA:]
    wq = p["w_q_b"][l].reshape(Q_LORA, N_HEADS, QK_DIM)
    wq_rope = wq[..., NOPE_DIM:]
    wkv = p["w_kv_b"][l].reshape(KV_LORA, N_HEADS, NOPE_DIM + V_DIM)
    wk = wkv[..., :NOPE_DIM]
    wv = wkv[..., NOPE_DIM:]
    zeros = jnp.zeros((QK_PAD - QK_DIM,), F32)

    def head_gain(g_nope, g_rope):
        return jnp.concatenate([g_nope[l], g_rope[l], g_rope[l], zeros])[None, :]

    out = {
        "mix_norm": row(p["mix_norm"]),
        "w_in": jnp.concatenate([w_in, _rot_half_cols(w_pe)], axis=-1).astype(BF16),
        "q_a_norm": row(p["q_a_norm"]),
        "wq": jnp.concatenate([wq, _rot_half_cols(wq_rope)], axis=-1)
              .reshape(Q_LORA, N_HEADS * QK_PAD).astype(BF16),
        "kv_a_norm": row(p["kv_a_norm"]),
        "wk": wk.reshape(KV_LORA, N_HEADS * NOPE_DIM).astype(BF16),
        "wv": wv.reshape(KV_LORA, N_HEADS * V_DIM).astype(BF16),
        "wk3": wk.transpose(1, 0, 2).astype(BF16),
        "wkt": wk.reshape(KV_LORA, N_HEADS * NOPE_DIM).T.astype(BF16),
        "wv3": wv.transpose(1, 0, 2).astype(BF16),
        "gq": head_gain(p["q_norm_nope"], p["q_norm_rope"]) * ATTN_SCALE,
        "gk": head_gain(p["k_norm_nope"], p["k_norm_rope"]),
        "pool_w": p["pool_w"][l].astype(BF16),
        "pool_scale": row(p["pool_scale"]),
        "pool_out_norm": row(p["pool_out_norm"]),
        "attn_out_norm": row(p["attn_out_norm"]),
        "w_out": p["w_out"][l].astype(BF16),
    }
    for name in ("ffn1", "ffn2"):
        out[name] = (row(p[name + "_norm"]), p[name + "_w_gate"][l].astype(BF16),
                     p[name + "_w_up"][l].astype(BF16), p[name + "_w_down"][l].astype(BF16))
    return out


def kernel(x_prompt, x_sample, cache_ckv, cache_kpe, state_pool, page_table, meta_tokens, ffn1_norm, ffn1_w_gate, ffn1_w_up, ffn1_w_down, mix_norm, w_in, pool_w, pool_scale, q_a_norm, w_q_b, kv_a_norm, w_kv_b, q_norm_nope, q_norm_rope, k_norm_nope, k_norm_rope, pool_out_norm, attn_out_norm, w_out, ffn2_norm, ffn2_w_gate, ffn2_w_up, ffn2_w_down):
    p = dict(ffn1_norm=ffn1_norm, ffn1_w_gate=ffn1_w_gate, ffn1_w_up=ffn1_w_up, ffn1_w_down=ffn1_w_down,
             mix_norm=mix_norm, w_in=w_in, pool_w=pool_w, pool_scale=pool_scale, q_a_norm=q_a_norm,
             w_q_b=w_q_b, kv_a_norm=kv_a_norm, w_kv_b=w_kv_b, q_norm_nope=q_norm_nope,
             q_norm_rope=q_norm_rope, k_norm_nope=k_norm_nope, k_norm_rope=k_norm_rope,
             pool_out_norm=pool_out_norm, attn_out_norm=attn_out_norm, w_out=w_out,
             ffn2_norm=ffn2_norm, ffn2_w_gate=ffn2_w_gate, ffn2_w_up=ffn2_w_up, ffn2_w_down=ffn2_w_down)
    depth = w_in.shape[0]
    batch, seq, _ = x_prompt.shape
    dec_batch, dec_seq, _ = x_sample.shape
    n_pages = page_table.shape[1]
    t_real = N_META + seq
    t_pad = -(-t_real // ATT_BLOCK) * ATT_BLOCK
    n_p = batch * t_pad
    n_s = dec_seq * dec_batch
    assert n_p % 544 == 0 and dec_batch % 8 == 0 and dec_seq * N_HEADS % 8 == 0

    meta = jnp.broadcast_to(meta_tokens[None].astype(F32), (batch, N_META, D_MODEL))
    xp = jnp.concatenate([meta, x_prompt, jnp.zeros((batch, t_pad - t_real, D_MODEL), F32)], axis=1)
    xp = xp.reshape(n_p, D_MODEL)
    xs = x_sample.transpose(1, 0, 2).reshape(n_s, D_MODEL)
    cs_p = jnp.tile(_rope_table(jnp.arange(t_pad)), (batch, 1))
    cs_s = jnp.repeat(_rope_table(n_pages * PAGE_SIZE + jnp.arange(dec_seq)), dec_batch, axis=0)

    outs = [[] for _ in range(6)]
    for l in range(depth):
        w = _layer_weights(l, p)
        xp = _ffn(xp, *w["ffn1"], tm=544)
        xs = _ffn(xs, *w["ffn1"], tm=n_s)

        u_p, ckv_p, kpe_p, q_p, k_p, v_p = _inproj(xp, cs_p, w, tm=ATT_BLOCK)
        attn_p = _flash(q_p, k_p, v_p, batch=batch, t_pad=t_pad)
        xp = _merge_prompt(xp, u_p, attn_p, w, batch=batch, t_pad=t_pad, tm=ATT_BLOCK)
        outs[0].append(ckv_p.reshape(batch, t_pad, KV_LORA)[:, :t_real])
        outs[1].append(kpe_p.reshape(batch, t_pad, 2 * ROPE_DIM)[:, :t_real, :ROPE_DIM])
        outs[2].append(u_p.reshape(batch, t_pad, POOL_WIDTH)[:, t_real - POOL_STATE:t_real])

        u_s, ckv_s, kpe_s, q_s, _, _ = _inproj(xs, cs_s, w, tm=ATT_BLOCK)
        qa, qr = _qabs(q_s, w)

        def per_seq(t):
            return t.reshape(N_HEADS, dec_seq, dec_batch, -1).transpose(2, 1, 0, 3) \
                    .reshape(dec_batch, dec_seq * N_HEADS, -1)

        c_new = ckv_s.reshape(dec_seq, dec_batch, KV_LORA).transpose(1, 0, 2)
        p_new = kpe_s[:, :ROPE_DIM].reshape(dec_seq, dec_batch, ROPE_DIM).transpose(1, 0, 2)
        pad = ((0, 0), (0, PAGE_SIZE - dec_seq), (0, 0))
        pc = _sattn(page_table, per_seq(qa), per_seq(qr[..., :ROPE_DIM]),
                    jnp.pad(c_new, pad).astype(BF16), jnp.pad(p_new, pad),
                    w["wkt"], cache_ckv, cache_kpe, layer=l)
        pc = pc.reshape(dec_batch, dec_seq, N_HEADS, KV_LORA).transpose(2, 1, 0, 3) \
               .reshape(N_HEADS, n_s, KV_LORA)
        ue = jnp.concatenate([state_pool[l].astype(F32).transpose(1, 0, 2),
                              u_s.reshape(dec_seq, dec_batch, POOL_WIDTH)], axis=0)
        xs = _merge_sample(xs, ue, pc, w, dec_seq=dec_seq, dec_batch=dec_batch)
        outs[3].append(c_new)
        outs[4].append(p_new)
        outs[5].append(ue[-POOL_STATE:].transpose(1, 0, 2))

        xp = _ffn(xp, *w["ffn2"], tm=544)
        xs = _ffn(xs, *w["ffn2"], tm=n_s)

    y_prompt = xp.reshape(batch, t_pad, D_MODEL)[:, N_META:t_real]
    y_sample = xs.reshape(dec_seq, dec_batch, D_MODEL).transpose(1, 0, 2)
    return (y_prompt, y_sample) + tuple(jnp.stack(o) for o in outs)
```

```python
import functools

import jax
import jax.numpy as jnp
from jax import lax
from jax.experimental import pallas as pl
from jax.experimental.pallas import tpu as pltpu

F32 = jnp.float32
BF16 = jnp.bfloat16

D_MODEL = 2048
N_META = 16
POOL_WIDTH = 1024
POOL_WINDOWS = (2, 4, 8, 16)
POOL_GROUP_DIM = 256
POOL_STATE = 15
N_HEADS = 8
NOPE_DIM = 128
ROPE_DIM = 64
ROPE_HALF = 32
QK_DIM = 192
QK_PAD = 256
V_DIM = 128
ATTN_WIDTH = 1024
Q_LORA = 512
KV_LORA = 256
D_FF = 5632
ROPE_THETA = 10000.0
EPS = 1e-6
ATTN_SCALE = QK_DIM ** -0.5
NEG_INF = -1e30
PAGE_SIZE = 128

IN_EXT = POOL_WIDTH + Q_LORA + KV_LORA + 2 * ROPE_DIM
FF_TILE = 512
FFN_ROWS = 1088
ATT_BLOCK = 256
FLASH_HEADS = 4
HALO = 16
KEY_CHUNK = 256
CHUNK_GROUP = 4
VMEM_LIMIT = 56 * 1024 * 1024


def _rms(x, g):
    return x * lax.rsqrt(jnp.mean(x * x, axis=-1, keepdims=True) + EPS) * g


def _dot(a, b):
    return jnp.dot(a, b, preferred_element_type=F32)


def _dot_nt(a, b):
    return lax.dot_general(a, b, (((1,), (1,)), ((), ())), preferred_element_type=F32)


def _params(sem, vmem=VMEM_LIMIT):
    return pltpu.CompilerParams(dimension_semantics=sem, vmem_limit_bytes=vmem)


def _ffn_kernel(x_ref, g_ref, wg_ref, wu_ref, wd_ref, o_ref, h_ref):
    j = pl.program_id(1)

    @pl.when(j == 0)
    def _():
        x = x_ref[...]
        h_ref[...] = _rms(x, g_ref[...]).astype(BF16)
        o_ref[...] = x

    h = h_ref[...]
    gate = _dot(h, wg_ref[...])
    up = _dot(h, wu_ref[...])
    a = (0.5 * gate * jax.nn.sigmoid(gate) * up).astype(BF16)
    o_ref[...] += _dot(a, wd_ref[...])


def _ffn(x, g, wg, wu, wd, *, layer, tm):
    n = x.shape[0]
    return pl.pallas_call(
        _ffn_kernel,
        out_shape=jax.ShapeDtypeStruct((n, D_MODEL), F32),
        grid=(n // tm, D_FF // FF_TILE),
        in_specs=[
            pl.BlockSpec((tm, D_MODEL), lambda i, j: (i, 0), pipeline_mode=pl.Buffered(1)),
            pl.BlockSpec((1, D_MODEL), lambda i, j: (0, 0)),
            pl.BlockSpec((None, D_MODEL, FF_TILE), lambda i, j: (layer, 0, j)),
            pl.BlockSpec((None, D_MODEL, FF_TILE), lambda i, j: (layer, 0, j)),
            pl.BlockSpec((None, FF_TILE, D_MODEL), lambda i, j: (layer, j, 0)),
        ],
        out_specs=pl.BlockSpec((tm, D_MODEL), lambda i, j: (i, 0)),
        scratch_shapes=[pltpu.VMEM((tm, D_MODEL), BF16)],
        compiler_params=_params(("parallel", "arbitrary")),
        name="ffn",
    )(x, g, wg, wu, wd)


def _inproj_kernel(x_ref, gmix_ref, win_ref, gqa_ref, wq_ref, gkva_ref, wk_ref, wv_ref,
                   cs_ref, gq_ref, gk_ref,
                   u_ref, ckv_ref, kpe_ref, q_ref, k_ref, v_ref):
    h = _rms(x_ref[...], gmix_ref[...]).astype(BF16)
    z = _dot(h, win_ref[...])
    u_ref[...] = z[:, :POOL_WIDTH]
    ql = _rms(z[:, POOL_WIDTH:POOL_WIDTH + Q_LORA], gqa_ref[...]).astype(BF16)
    o_kv = POOL_WIDTH + Q_LORA
    c = _rms(z[:, o_kv:o_kv + KV_LORA], gkva_ref[...])
    ckv_ref[...] = c

    cs = cs_ref[...]
    lane = lax.broadcasted_iota(jnp.int32, cs.shape, 1)
    low = lane < ROPE_DIM

    def rope(t):
        t = t * cs
        return jnp.where(low, t + pltpu.roll(t, ROPE_DIM, axis=1), 0.0)

    kpe = rope(z[:, o_kv + KV_LORA:])
    kpe_ref[...] = kpe
    kpe_ss = jnp.sum(kpe * kpe, axis=-1, keepdims=True)

    q = _dot(ql, wq_ref[...])
    cb = c.astype(BF16)
    kn = _dot(cb, wk_ref[...])
    v = _dot(cb, wv_ref[...])
    gq = gq_ref[...]
    gk = gk_ref[...]
    inv_d = 1.0 / QK_DIM
    for hd in range(N_HEADS):
        qa = q[:, hd * QK_PAD:hd * QK_PAD + NOPE_DIM]
        qb = rope(q[:, hd * QK_PAD + NOPE_DIM:(hd + 1) * QK_PAD])
        rs = lax.rsqrt(jnp.sum(qa * qa + qb * qb, axis=-1, keepdims=True) * inv_d + EPS)
        q_ref[hd, :, :NOPE_DIM] = (qa * rs * gq[:, :NOPE_DIM]).astype(BF16)
        q_ref[hd, :, NOPE_DIM:] = (qb * rs * gq[:, NOPE_DIM:]).astype(BF16)
        ka = kn[:, hd * NOPE_DIM:(hd + 1) * NOPE_DIM]
        rk = lax.rsqrt((jnp.sum(ka * ka, axis=-1, keepdims=True) + kpe_ss) * inv_d + EPS)
        k_ref[hd, :, :NOPE_DIM] = (ka * rk * gk[:, :NOPE_DIM]).astype(BF16)
        k_ref[hd, :, NOPE_DIM:] = (kpe * rk * gk[:, NOPE_DIM:]).astype(BF16)
        v_ref[hd] = v[:, hd * V_DIM:(hd + 1) * V_DIM].astype(BF16)


def _inproj(x, cs, w, *, tm):
    n = x.shape[0]
    row = lambda i: (i, 0)
    fix = lambda i: (0, 0)
    head = lambda i: (0, i, 0)
    return pl.pallas_call(
        _inproj_kernel,
        out_shape=(
            jax.ShapeDtypeStruct((n, POOL_WIDTH), F32),
            jax.ShapeDtypeStruct((n, KV_LORA), F32),
            jax.ShapeDtypeStruct((n, 2 * ROPE_DIM), F32),
            jax.ShapeDtypeStruct((N_HEADS, n, QK_PAD), BF16),
            jax.ShapeDtypeStruct((N_HEADS, n, QK_PAD), BF16),
            jax.ShapeDtypeStruct((N_HEADS, n, V_DIM), BF16),
        ),
        grid=(n // tm,),
        in_specs=[
            pl.BlockSpec((tm, D_MODEL), row),
            pl.BlockSpec((1, D_MODEL), fix),
            pl.BlockSpec((D_MODEL, IN_EXT), fix),
            pl.BlockSpec((1, Q_LORA), fix),
            pl.BlockSpec((Q_LORA, N_HEADS * QK_PAD), fix),
            pl.BlockSpec((1, KV_LORA), fix),
            pl.BlockSpec((KV_LORA, N_HEADS * NOPE_DIM), fix),
            pl.BlockSpec((KV_LORA, N_HEADS * V_DIM), fix),
            pl.BlockSpec((tm, 2 * ROPE_DIM), row),
            pl.BlockSpec((1, QK_PAD), fix),
            pl.BlockSpec((1, QK_PAD), fix),
        ],
        out_specs=(
            pl.BlockSpec((tm, POOL_WIDTH), row),
            pl.BlockSpec((tm, KV_LORA), row),
            pl.BlockSpec((tm, 2 * ROPE_DIM), row),
            pl.BlockSpec((N_HEADS, tm, QK_PAD), head),
            pl.BlockSpec((N_HEADS, tm, QK_PAD), head),
            pl.BlockSpec((N_HEADS, tm, V_DIM), head),
        ),
        compiler_params=_params(("parallel",)),
        name="inproj",
    )(x, w["mix_norm"], w["w_in"], w["q_a_norm"], w["wq"], w["kv_a_norm"], w["wk"], w["wv"],
      cs, w["gq"], w["gk"])


def _flash_kernel(q_ref, k_ref, v_ref, o_ref):
    i = pl.program_id(2)
    nh, bq, _ = q_ref.shape

    def step(hd, off, carry, mask):
        m, l, acc = carry
        s = _dot_nt(q_ref[hd], k_ref[hd, pl.ds(off, ATT_BLOCK), :])
        if mask is not None:
            s = jnp.where(mask, s, NEG_INF)
        m_new = jnp.maximum(m, jnp.max(s, axis=-1, keepdims=True))
        p = jnp.exp(s - m_new)
        alpha = jnp.exp(m - m_new)
        l = alpha * l + jnp.sum(p, axis=-1, keepdims=True)
        acc = alpha * acc + _dot(p.astype(BF16), v_ref[hd, pl.ds(off, ATT_BLOCK), :])
        return m_new, l, acc

    def body(j, carry):
        off = pl.multiple_of(j * ATT_BLOCK, ATT_BLOCK)
        return tuple(step(hd, off, carry[hd], None) for hd in range(nh))

    init = (jnp.full((bq, 1), -jnp.inf, F32), jnp.zeros((bq, 1), F32), jnp.zeros((bq, V_DIM), F32))
    carry = lax.fori_loop(0, i, body, (init,) * nh)
    off = pl.multiple_of(i * ATT_BLOCK, ATT_BLOCK)
    row = lax.broadcasted_iota(jnp.int32, (bq, ATT_BLOCK), 0)
    col = lax.broadcasted_iota(jnp.int32, (bq, ATT_BLOCK), 1)
    for hd in range(nh):
        _, l, acc = step(hd, off, carry[hd], col <= row)
        o_ref[hd] = acc / l


def _flash(q, k, v, *, batch, t_pad):
    n = q.shape[1]
    nq = t_pad // ATT_BLOCK
    qmap = lambda b, h, i: (h, b * nq + i, 0)
    kvmap = lambda b, h, i: (h, b, 0)
    return pl.pallas_call(
        _flash_kernel,
        out_shape=jax.ShapeDtypeStruct((N_HEADS, n, V_DIM), F32),
        grid=(batch, N_HEADS // FLASH_HEADS, nq),
        in_specs=[
            pl.BlockSpec((FLASH_HEADS, ATT_BLOCK, QK_PAD), qmap),
            pl.BlockSpec((FLASH_HEADS, t_pad, QK_PAD), kvmap),
            pl.BlockSpec((FLASH_HEADS, t_pad, V_DIM), kvmap),
        ],
        out_specs=pl.BlockSpec((FLASH_HEADS, ATT_BLOCK, V_DIM), qmap),
        compiler_params=_params(("parallel", "parallel", "arbitrary")),
        name="flash",
    )(q, k, v)


def _merge_tail(x, d, a, pw_ref, ps_ref, pn_ref, an_ref, wo_ref):
    pool = jnp.concatenate(
        [_dot(d[:, g * POOL_GROUP_DIM:(g + 1) * POOL_GROUP_DIM].astype(BF16), pw_ref[g])
         for g in range(len(POOL_WINDOWS))], axis=-1) * ps_ref[...]
    cat = jnp.concatenate([_rms(pool, pn_ref[...]).astype(BF16),
                           _rms(a, an_ref[...]).astype(BF16)], axis=-1)
    return x + _dot(cat, wo_ref[...])


def _merge_prompt_kernel(x_ref, u_ref, halo_ref, a_ref, pw_ref, ps_ref, pn_ref, an_ref, wo_ref,
                         o_ref, ext_ref):
    i = pl.program_id(1)
    tm = u_ref.shape[0]
    ext_ref[0:HALO, :] = jnp.where(i > 0, halo_ref[...], 0.0)
    ext_ref[HALO:HALO + tm, :] = u_ref[...]
    pos = i * tm + lax.broadcasted_iota(jnp.int32, (tm, 1), 0)
    ds = []
    for g, w in enumerate(POOL_WINDOWS):
        sl = slice(g * POOL_GROUP_DIM, (g + 1) * POOL_GROUP_DIM)
        tok = ext_ref[HALO:HALO + tm, sl]
        acc = tok
        for k in range(1, w):
            acc = acc + ext_ref[HALO - k:HALO - k + tm, sl]
        cnt = jnp.minimum(pos + 1, w).astype(F32)
        ds.append(acc / cnt - tok)
    d = jnp.concatenate(ds, axis=-1)
    a = jnp.concatenate([a_ref[hd] for hd in range(N_HEADS)], axis=-1)
    o_ref[...] = _merge_tail(x_ref[...], d, a, pw_ref, ps_ref, pn_ref, an_ref, wo_ref)


def _merge_weight_specs(fix2, fix3):
    return [
        pl.BlockSpec((len(POOL_WINDOWS), POOL_GROUP_DIM, POOL_GROUP_DIM), fix3),
        pl.BlockSpec((1, POOL_WIDTH), fix2),
        pl.BlockSpec((1, POOL_WIDTH), fix2),
        pl.BlockSpec((1, ATTN_WIDTH), fix2),
        pl.BlockSpec((D_MODEL, D_MODEL), fix2),
    ]


def _merge_prompt(x, u, attn, w, *, batch, t_pad, tm):
    n = x.shape[0]
    nt = t_pad // tm
    row = lambda b, i: (b * nt + i, 0)
    return pl.pallas_call(
        _merge_prompt_kernel,
        out_shape=jax.ShapeDtypeStruct((n, D_MODEL), F32),
        grid=(batch, nt),
        in_specs=[
            pl.BlockSpec((tm, D_MODEL), row),
            pl.BlockSpec((tm, POOL_WIDTH), row),
            pl.BlockSpec((HALO, POOL_WIDTH),
                         lambda b, i: (jnp.maximum((b * nt + i) * (tm // HALO) - 1, 0), 0)),
            pl.BlockSpec((N_HEADS, tm, V_DIM), lambda b, i: (0, b * nt + i, 0)),
        ] + _merge_weight_specs(lambda b, i: (0, 0), lambda b, i: (0, 0, 0)),
        out_specs=pl.BlockSpec((tm, D_MODEL), row),
        scratch_shapes=[pltpu.VMEM((HALO + tm, POOL_WIDTH), F32)],
        compiler_params=_params(("parallel", "arbitrary")),
        name="merge_prompt",
    )(x, u, u, attn, w["pool_w"], w["pool_scale"], w["pool_out_norm"], w["attn_out_norm"], w["w_out"])


def _merge_sample_kernel(x_ref, ue_ref, pc_ref, wv_ref, pw_ref, ps_ref, pn_ref, an_ref, wo_ref, o_ref):
    s = pl.program_id(0)
    ds = []
    for g, w in enumerate(POOL_WINDOWS):
        sl = slice(g * POOL_GROUP_DIM, (g + 1) * POOL_GROUP_DIM)
        tok = ue_ref[POOL_STATE + s, :, sl]
        acc = tok
        for k in range(1, w):
            acc = acc + ue_ref[POOL_STATE + s - k, :, sl]
        ds.append(acc * (1.0 / w) - tok)
    d = jnp.concatenate(ds, axis=-1)
    a = jnp.concatenate([_dot(pc_ref[hd].astype(BF16), wv_ref[hd]) for hd in range(N_HEADS)], axis=-1)
    o_ref[...] = _merge_tail(x_ref[...], d, a, pw_ref, ps_ref, pn_ref, an_ref, wo_ref)


def _merge_sample(x, ue, pc, w, *, dec_seq, dec_batch):
    n = x.shape[0]
    n_ext = ue.shape[0]
    return pl.pallas_call(
        _merge_sample_kernel,
        out_shape=jax.ShapeDtypeStruct((n, D_MODEL), F32),
        grid=(dec_seq,),
        in_specs=[
            pl.BlockSpec((dec_batch, D_MODEL), lambda s: (s, 0)),
            pl.BlockSpec((n_ext, dec_batch, POOL_WIDTH), lambda s: (0, 0, 0)),
            pl.BlockSpec((N_HEADS, dec_batch, KV_LORA), lambda s: (0, s, 0)),
            pl.BlockSpec((N_HEADS, KV_LORA, V_DIM), lambda s: (0, 0, 0)),
        ] + _merge_weight_specs(lambda s: (0, 0), lambda s: (0, 0, 0)),
        out_specs=pl.BlockSpec((dec_batch, D_MODEL), lambda s: (s, 0)),
        compiler_params=_params(("arbitrary",)),
        name="merge_sample",
    )(x, ue, pc, w["wv3"], w["pool_w"], w["pool_scale"], w["pool_out_norm"], w["attn_out_norm"], w["w_out"])


def _qabs_kernel(q_ref, wk_ref, gk_ref, qa_ref, qr_ref):
    q = q_ref[...].astype(F32)
    gk = gk_ref[...]
    qn = (q[:, :NOPE_DIM] * gk[:, :NOPE_DIM]).astype(BF16)
    qa_ref[...] = _dot_nt(qn, wk_ref[...]).astype(BF16)
    qr_ref[...] = (q[:, NOPE_DIM:] * gk[:, NOPE_DIM:]).astype(BF16)


def _qabs(q, w):
    n = q.shape[1]
    return pl.pallas_call(
        _qabs_kernel,
        out_shape=(jax.ShapeDtypeStruct((N_HEADS, n, KV_LORA), BF16),
                   jax.ShapeDtypeStruct((N_HEADS, n, QK_PAD - NOPE_DIM), BF16)),
        grid=(N_HEADS,),
        in_specs=[
            pl.BlockSpec((None, n, QK_PAD), lambda h: (h, 0, 0)),
            pl.BlockSpec((None, KV_LORA, NOPE_DIM), lambda h: (h, 0, 0)),
            pl.BlockSpec((1, QK_PAD), lambda h: (0, 0)),
        ],
        out_specs=(pl.BlockSpec((None, n, KV_LORA), lambda h: (h, 0, 0)),
                   pl.BlockSpec((None, n, QK_PAD - NOPE_DIM), lambda h: (h, 0, 0))),
        compiler_params=_params(("parallel",)),
        name="qabs",
    )(q, w["wk3"], w["gk"])


def _sattn_kernel(pt_ref, qa_ref, qr_ref, cnew_ref, pnew_ref, wkt_ref, ckv_hbm, kpe_hbm, o_ref,
                  a_ref, cbuf, pbuf, cb16, s_all, sem, *, layer, n_pages, n_rows):
    b = pl.program_id(0)
    nb = pl.num_programs(0)
    slot = b % 2
    n_kn = N_HEADS * NOPE_DIM

    def page_copies(page, p, sl):
        rows = pl.ds(pl.multiple_of(p * PAGE_SIZE, PAGE_SIZE), PAGE_SIZE)
        return (pltpu.make_async_copy(ckv_hbm.at[layer, page], cbuf.at[sl, rows], sem.at[0, sl]),
                pltpu.make_async_copy(kpe_hbm.at[layer, page], pbuf.at[sl, p], sem.at[1, sl]))

    def issue(seq, sl):
        def body(p, carry):
            for cp in page_copies(pt_ref[seq, p], p, sl):
                cp.start()
            return carry
        lax.fori_loop(0, n_pages, body, 0)

    def wait(sl):
        def body(p, carry):
            for cp in page_copies(0, p, sl):
                cp.wait()
            return carry
        lax.fori_loop(0, n_pages, body, 0)

    @pl.when(b == 0)
    def _():
        a_ref[0:n_kn, :] = wkt_ref[...]
        issue(0, 0)

    @pl.when(b + 1 < nb)
    def _():
        issue(b + 1, 1 - slot)

    a_ref[n_kn:n_kn + n_rows, :] = qa_ref[...]
    qr = qr_ref[...]
    n_chunks = n_pages * PAGE_SIZE // KEY_CHUNK
    pages_per_chunk = KEY_CHUNK // PAGE_SIZE

    def scores(cb, pt):
        nk = cb.shape[0]
        r_all = _dot_nt(a_ref[...], cb)
        kn = r_all[:n_kn]
        ssq = jnp.sum((kn * kn).reshape(N_HEADS, NOPE_DIM, nk), axis=1)
        pe2 = jnp.sum(pt * pt, axis=0, keepdims=True)
        r = lax.rsqrt((ssq + pe2) * (1.0 / QK_DIM) + EPS)
        raw_r = _dot(qr, pt.astype(BF16))
        return (r_all[n_kn:] + raw_r) * jnp.concatenate([r] * (n_rows // N_HEADS), axis=0)

    wait(slot)

    def body(g, carry):
        for j in range(CHUNK_GROUP):
            c = g * CHUNK_GROUP + j
            rows = pl.ds(pl.multiple_of(c * KEY_CHUNK, KEY_CHUNK), KEY_CHUNK)
            cb = cbuf[slot, rows, :].astype(BF16)
            cb16[rows, :] = cb
            pt = jnp.concatenate([pbuf[slot, c * pages_per_chunk + k] for k in range(pages_per_chunk)],
                                 axis=1)
            s_all[c] = scores(cb, pt)
        return carry

    lax.fori_loop(0, n_chunks // CHUNK_GROUP, body, 0)
    qpos = lax.broadcasted_iota(jnp.int32, (n_rows, KEY_CHUNK), 0) // N_HEADS
    kidx = lax.broadcasted_iota(jnp.int32, (n_rows, KEY_CHUNK), 1)
    cb16[n_chunks * KEY_CHUNK:, :] = cnew_ref[...]
    s_all[n_chunks] = jnp.where(kidx <= qpos, scores(cnew_ref[...], pnew_ref[...]), NEG_INF)

    s = s_all[...]
    m = jnp.max(jnp.max(s, axis=0), axis=-1, keepdims=True)
    p = jnp.exp(s - m[None])
    l = jnp.sum(jnp.sum(p, axis=0), axis=-1, keepdims=True)
    p2 = jnp.concatenate([p[c] for c in range(n_chunks + 1)], axis=1).astype(BF16)
    o_ref[...] = _dot(p2, cb16[...]) / l


def _sattn(page_table, qa, qr, cnew, pnew, wkt, cache_ckv, cache_kpe_t, *, layer):
    nb, n_pages = page_table.shape
    n_rows = qa.shape[1]
    n_keys = n_pages * PAGE_SIZE
    n_chunks = n_keys // KEY_CHUNK
    assert n_chunks % CHUNK_GROUP == 0 and KEY_CHUNK % PAGE_SIZE == 0
    seq3 = lambda b, pt: (b, 0, 0)
    grid_spec = pltpu.PrefetchScalarGridSpec(
        num_scalar_prefetch=1,
        grid=(nb,),
        in_specs=[
            pl.BlockSpec((None, n_rows, KV_LORA), seq3),
            pl.BlockSpec((None, n_rows, ROPE_DIM), seq3),
            pl.BlockSpec((None, KEY_CHUNK, KV_LORA), seq3),
            pl.BlockSpec((None, ROPE_DIM, KEY_CHUNK), seq3),
            pl.BlockSpec((N_HEADS * NOPE_DIM, KV_LORA), lambda b, pt: (0, 0)),
            pl.BlockSpec(memory_space=pl.ANY),
            pl.BlockSpec(memory_space=pl.ANY),
        ],
        out_specs=pl.BlockSpec((None, n_rows, KV_LORA), seq3),
        scratch_shapes=[
            pltpu.VMEM((N_HEADS * NOPE_DIM + n_rows, KV_LORA), BF16),
            pltpu.VMEM((2, n_keys, KV_LORA), F32),
            pltpu.VMEM((2, n_pages, ROPE_DIM, PAGE_SIZE), F32),
            pltpu.VMEM((n_keys + KEY_CHUNK, KV_LORA), BF16),
            pltpu.VMEM((n_chunks + 1, n_rows, KEY_CHUNK), F32),
            pltpu.SemaphoreType.DMA((2, 2)),
        ],
    )
    return pl.pallas_call(
        functools.partial(_sattn_kernel, layer=layer, n_pages=n_pages, n_rows=n_rows),
        out_shape=jax.ShapeDtypeStruct((nb, n_rows, KV_LORA), F32),
        grid_spec=grid_spec,
        compiler_params=_params(("arbitrary",)),
        name="sattn",
    )(page_table, qa, qr, cnew, pnew, wkt, cache_ckv, cache_kpe_t)


def _rot_half_cols(w):
    return jnp.concatenate([-w[..., ROPE_HALF:], w[..., :ROPE_HALF]], axis=-1)


def _rope_table(pos):
    inv = ROPE_THETA ** (-jnp.arange(ROPE_HALF, dtype=F32) / ROPE_HALF)
    ang = pos.astype(F32)[:, None] * inv[None, :]
    c, s = jnp.cos(ang), jnp.sin(ang)
    return jnp.concatenate([c, c, s, s], axis=-1)


def _layer_weights(l, p):
    row = lambda v: v[l][None, :].astype(F32)
    w_in = p["w_in"][l]
    w_pe = w_in[:, POOL_WIDTH + Q_LORA + KV_LORA:]
    wq = p["w_q_b"][l].reshape(Q_LORA, N_HEADS, QK_DIM)
    wq_rope = wq[..., NOPE_DIM:]
    wkv = p["w_kv_b"][l].reshape(KV_LORA, N_HEADS, NOPE_DIM + V_DIM)
    wk = wkv[..., :NOPE_DIM]
    wv = wkv[..., NOPE_DIM:]
    zeros = jnp.zeros((QK_PAD - QK_DIM,), F32)

    def head_gain(g_nope, g_rope):
        return jnp.concatenate([g_nope[l], g_rope[l], g_rope[l], zeros])[None, :]

    out = {
        "mix_norm": row(p["mix_norm"]),
        "w_in": jnp.concatenate([w_in, _rot_half_cols(w_pe)], axis=-1).astype(BF16),
        "q_a_norm": row(p["q_a_norm"]),
        "wq": jnp.concatenate([wq, _rot_half_cols(wq_rope)], axis=-1)
              .reshape(Q_LORA, N_HEADS * QK_PAD).astype(BF16),
        "kv_a_norm": row(p["kv_a_norm"]),
        "wk": wk.reshape(KV_LORA, N_HEADS * NOPE_DIM).astype(BF16),
        "wv": wv.reshape(KV_LORA, N_HEADS * V_DIM).astype(BF16),
        "wk3": wk.transpose(1, 0, 2).astype(BF16),
        "wkt": wk.reshape(KV_LORA, N_HEADS * NOPE_DIM).T.astype(BF16),
        "wv3": wv.transpose(1, 0, 2).astype(BF16),
        "gq": head_gain(p["q_norm_nope"], p["q_norm_rope"]) * ATTN_SCALE,
        "gk": head_gain(p["k_norm_nope"], p["k_norm_rope"]),
        "pool_w": p["pool_w"][l].astype(BF16),
        "pool_scale": row(p["pool_scale"]),
        "pool_out_norm": row(p["pool_out_norm"]),
        "attn_out_norm": row(p["attn_out_norm"]),
        "w_out": p["w_out"][l].astype(BF16),
        "ffn1_norm": row(p["ffn1_norm"]),
        "ffn2_norm": row(p["ffn2_norm"]),
    }
    return out


def kernel(x_prompt, x_sample, cache_ckv, cache_kpe, state_pool, page_table, meta_tokens, ffn1_norm, ffn1_w_gate, ffn1_w_up, ffn1_w_down, mix_norm, w_in, pool_w, pool_scale, q_a_norm, w_q_b, kv_a_norm, w_kv_b, q_norm_nope, q_norm_rope, k_norm_nope, k_norm_rope, pool_out_norm, attn_out_norm, w_out, ffn2_norm, ffn2_w_gate, ffn2_w_up, ffn2_w_down):
    p = dict(ffn1_norm=ffn1_norm, ffn1_w_gate=ffn1_w_gate, ffn1_w_up=ffn1_w_up, ffn1_w_down=ffn1_w_down,
             mix_norm=mix_norm, w_in=w_in, pool_w=pool_w, pool_scale=pool_scale, q_a_norm=q_a_norm,
             w_q_b=w_q_b, kv_a_norm=kv_a_norm, w_kv_b=w_kv_b, q_norm_nope=q_norm_nope,
             q_norm_rope=q_norm_rope, k_norm_nope=k_norm_nope, k_norm_rope=k_norm_rope,
             pool_out_norm=pool_out_norm, attn_out_norm=attn_out_norm, w_out=w_out,
             ffn2_norm=ffn2_norm, ffn2_w_gate=ffn2_w_gate, ffn2_w_up=ffn2_w_up, ffn2_w_down=ffn2_w_down)
    depth = w_in.shape[0]
    batch, seq, _ = x_prompt.shape
    dec_batch, dec_seq, _ = x_sample.shape
    n_pages = page_table.shape[1]
    t_real = N_META + seq
    t_pad = -(-t_real // ATT_BLOCK) * ATT_BLOCK
    n_p = batch * t_pad
    n_s = dec_seq * dec_batch
    assert n_p % FFN_ROWS == 0 and dec_batch % 8 == 0 and dec_seq * N_HEADS % 8 == 0
    ffn_w = {name: tuple(p[name + suffix].astype(BF16) for suffix in ("_w_gate", "_w_up", "_w_down"))
             for name in ("ffn1", "ffn2")}
    cache_kpe_t = jnp.swapaxes(cache_kpe, 2, 3)

    meta = jnp.broadcast_to(meta_tokens[None].astype(F32), (batch, N_META, D_MODEL))
    xp = jnp.concatenate([meta, x_prompt, jnp.zeros((batch, t_pad - t_real, D_MODEL), F32)], axis=1)
    xp = xp.reshape(n_p, D_MODEL)
    xs = x_sample.transpose(1, 0, 2).reshape(n_s, D_MODEL)
    cs_p = jnp.tile(_rope_table(jnp.arange(t_pad)), (batch, 1))
    cs_s = jnp.repeat(_rope_table(n_pages * PAGE_SIZE + jnp.arange(dec_seq)), dec_batch, axis=0)

    outs = [[] for _ in range(6)]
    for l in range(depth):
        w = _layer_weights(l, p)
        xp = _ffn(xp, w["ffn1_norm"], *ffn_w["ffn1"], layer=l, tm=FFN_ROWS)
        xs = _ffn(xs, w["ffn1_norm"], *ffn_w["ffn1"], layer=l, tm=n_s)

        u_p, ckv_p, kpe_p, q_p, k_p, v_p = _inproj(xp, cs_p, w, tm=ATT_BLOCK)
        attn_p = _flash(q_p, k_p, v_p, batch=batch, t_pad=t_pad)
        xp = _merge_prompt(xp, u_p, attn_p, w, batch=batch, t_pad=t_pad, tm=ATT_BLOCK)
        outs[0].append(ckv_p.reshape(batch, t_pad, KV_LORA)[:, :t_real])
        outs[1].append(kpe_p.reshape(batch, t_pad, 2 * ROPE_DIM)[:, :t_real, :ROPE_DIM])
        outs[2].append(u_p.reshape(batch, t_pad, POOL_WIDTH)[:, t_real - POOL_STATE:t_real])

        u_s, ckv_s, kpe_s, q_s, _, _ = _inproj(xs, cs_s, w, tm=ATT_BLOCK)
        qa, qr = _qabs(q_s, w)

        def per_seq(t):
            return t.reshape(N_HEADS, dec_seq, dec_batch, -1).transpose(2, 1, 0, 3) \
                    .reshape(dec_batch, dec_seq * N_HEADS, -1)

        c_new = ckv_s.reshape(dec_seq, dec_batch, KV_LORA).transpose(1, 0, 2)
        p_new = kpe_s[:, :ROPE_DIM].reshape(dec_seq, dec_batch, ROPE_DIM).transpose(1, 0, 2)
        n_fill = KEY_CHUNK - dec_seq
        pc = _sattn(page_table, per_seq(qa), per_seq(qr[..., :ROPE_DIM]),
                    jnp.pad(c_new, ((0, 0), (0, n_fill), (0, 0))).astype(BF16),
                    jnp.pad(p_new.transpose(0, 2, 1), ((0, 0), (0, 0), (0, n_fill))),
                    w["wkt"], cache_ckv, cache_kpe_t, layer=l)
        pc = pc.reshape(dec_batch, dec_seq, N_HEADS, KV_LORA).transpose(2, 1, 0, 3) \
               .reshape(N_HEADS, n_s, KV_LORA)
        ue = jnp.concatenate([state_pool[l].astype(F32).transpose(1, 0, 2),
                              u_s.reshape(dec_seq, dec_batch, POOL_WIDTH)], axis=0)
        xs = _merge_sample(xs, ue, pc, w, dec_seq=dec_seq, dec_batch=dec_batch)
        outs[3].append(c_new)
        outs[4].append(p_new)
        outs[5].append(ue[-POOL_STATE:].transpose(1, 0, 2))

        xp = _ffn(xp, w["ffn2_norm"], *ffn_w["ffn2"], layer=l, tm=FFN_ROWS)
        xs = _ffn(xs, w["ffn2_norm"], *ffn_w["ffn2"], layer=l, tm=n_s)

    y_prompt = xp.reshape(batch, t_pad, D_MODEL)[:, N_META:t_real]
    y_sample = xs.reshape(dec_seq, dec_batch, D_MODEL).transpose(1, 0, 2)
    return (y_prompt, y_sample) + tuple(jnp.stack(o) for o in outs)
```

```python
import functools

import jax
import jax.numpy as jnp
from jax import lax
from jax.experimental import pallas as pl
from jax.experimental.pallas import tpu as pltpu

F32 = jnp.float32
BF16 = jnp.bfloat16

D_MODEL = 2048
N_META = 16
POOL_WIDTH = 1024
POOL_WINDOWS = (2, 4, 8, 16)
POOL_GROUP_DIM = 256
POOL_STATE = 15
N_HEADS = 8
NOPE_DIM = 128
ROPE_DIM = 64
ROPE_HALF = 32
QK_DIM = 192
QK_PAD = 256
V_DIM = 128
ATTN_WIDTH = 1024
Q_LORA = 512
KV_LORA = 256
D_FF = 5632
ROPE_THETA = 10000.0
EPS = 1e-6
ATTN_SCALE = QK_DIM ** -0.5
LOG2_E = 1.4426950408889634
NEG_INF = -1e30
PAGE_SIZE = 128

IN_EXT = POOL_WIDTH + Q_LORA + KV_LORA + 2 * ROPE_DIM
FF_TILE = 512
FFN_ROWS = 544
ATT_BLOCK = 256
FLASH_HEADS = 4
HALO = 16
KEY_CHUNK = 256
CHUNK_GROUP = 8
VMEM_LIMIT = 56 * 1024 * 1024


def _rms(x, g):
    return x * lax.rsqrt(jnp.mean(x * x, axis=-1, keepdims=True) + EPS) * g


def _dot(a, b):
    return jnp.dot(a, b, preferred_element_type=F32)


def _dot_nt(a, b):
    return lax.dot_general(a, b, (((1,), (1,)), ((), ())), preferred_element_type=F32)


def _params(sem, vmem=VMEM_LIMIT, flags=None):
    return pltpu.CompilerParams(dimension_semantics=sem, vmem_limit_bytes=vmem, flags=flags)


def _ffn_kernel(x_ref, g_ref, wg_ref, wu_ref, wd_ref, o_ref, h_ref):
    j = pl.program_id(1)

    @pl.when(j == 0)
    def _():
        x = x_ref[...]
        h_ref[...] = _rms(x, g_ref[...]).astype(BF16)
        o_ref[...] = x

    h = h_ref[...]
    gate = _dot(h, wg_ref[...])
    up = _dot(h, wu_ref[...])
    a = (0.5 * gate * jax.nn.sigmoid(gate) * up).astype(BF16)
    o_ref[...] += _dot(a, wd_ref[...])


def _ffn(x, g, wg, wu, wd, *, layer, tm):
    n = x.shape[0]
    return pl.pallas_call(
        _ffn_kernel,
        out_shape=jax.ShapeDtypeStruct((n, D_MODEL), F32),
        grid=(n // tm, D_FF // FF_TILE),
        in_specs=[
            pl.BlockSpec((tm, D_MODEL), lambda i, j: (i, 0)),
            pl.BlockSpec((1, D_MODEL), lambda i, j: (0, 0)),
            pl.BlockSpec((None, D_MODEL, FF_TILE), lambda i, j: (layer, 0, j)),
            pl.BlockSpec((None, D_MODEL, FF_TILE), lambda i, j: (layer, 0, j)),
            pl.BlockSpec((None, FF_TILE, D_MODEL), lambda i, j: (layer, j, 0)),
        ],
        out_specs=pl.BlockSpec((tm, D_MODEL), lambda i, j: (i, 0)),
        scratch_shapes=[pltpu.VMEM((tm, D_MODEL), BF16)],
        compiler_params=_params(("parallel", "arbitrary")),
        name="ffn",
    )(x, g, wg, wu, wd)


def _inproj_kernel(x_ref, gmix_ref, win_ref, gqa_ref, wq_ref, gkva_ref, wk_ref, wv_ref,
                   cs_ref, gq_ref, gk_ref,
                   u_ref, ckv_ref, kpe_ref, q_ref, k_ref, v_ref):
    h = _rms(x_ref[...], gmix_ref[...]).astype(BF16)
    z = _dot(h, win_ref[...])
    u_ref[...] = z[:, :POOL_WIDTH]
    ql = _rms(z[:, POOL_WIDTH:POOL_WIDTH + Q_LORA], gqa_ref[...]).astype(BF16)
    o_kv = POOL_WIDTH + Q_LORA
    c = _rms(z[:, o_kv:o_kv + KV_LORA], gkva_ref[...])
    ckv_ref[...] = c

    cs = cs_ref[...]
    lane = lax.broadcasted_iota(jnp.int32, cs.shape, 1)
    low = lane < ROPE_DIM

    def rope(t):
        t = t * cs
        return jnp.where(low, t + pltpu.roll(t, ROPE_DIM, axis=1), 0.0)

    kpe = rope(z[:, o_kv + KV_LORA:])
    kpe_ref[...] = kpe
    kpe_ss = jnp.sum(kpe * kpe, axis=-1, keepdims=True)

    q = _dot(ql, wq_ref[...])
    cb = c.astype(BF16)
    kn = _dot(cb, wk_ref[...])
    vt = _dot_nt(wv_ref[...], cb)
    gq = gq_ref[...]
    gk = gk_ref[...]
    inv_d = 1.0 / QK_DIM
    for hd in range(N_HEADS):
        qa = q[:, hd * QK_PAD:hd * QK_PAD + NOPE_DIM]
        qb = rope(q[:, hd * QK_PAD + NOPE_DIM:(hd + 1) * QK_PAD])
        rs = lax.rsqrt(jnp.sum(qa * qa + qb * qb, axis=-1, keepdims=True) * inv_d + EPS)
        q_ref[hd, :, :NOPE_DIM] = (qa * rs * gq[:, :NOPE_DIM]).astype(BF16)
        q_ref[hd, :, NOPE_DIM:] = (qb * rs * gq[:, NOPE_DIM:]).astype(BF16)
        ka = kn[:, hd * NOPE_DIM:(hd + 1) * NOPE_DIM]
        rk = lax.rsqrt((jnp.sum(ka * ka, axis=-1, keepdims=True) + kpe_ss) * inv_d + EPS)
        k_ref[hd, :, :NOPE_DIM] = (ka * rk * gk[:, :NOPE_DIM]).astype(BF16)
        k_ref[hd, :, NOPE_DIM:] = (kpe * rk * gk[:, NOPE_DIM:]).astype(BF16)
        v_ref[hd] = vt[hd * V_DIM:(hd + 1) * V_DIM, :].astype(BF16)


def _inproj(x, cs, w):
    n = x.shape[0]
    tm = ATT_BLOCK
    row = lambda i: (i, 0)
    fix = lambda i: (0, 0)
    head = lambda i: (0, i, 0)
    return pl.pallas_call(
        _inproj_kernel,
        out_shape=(
            jax.ShapeDtypeStruct((n, POOL_WIDTH), F32),
            jax.ShapeDtypeStruct((n, KV_LORA), F32),
            jax.ShapeDtypeStruct((n, 2 * ROPE_DIM), F32),
            jax.ShapeDtypeStruct((N_HEADS, n, QK_PAD), BF16),
            jax.ShapeDtypeStruct((N_HEADS, n, QK_PAD), BF16),
            jax.ShapeDtypeStruct((N_HEADS, n // tm, V_DIM, tm), BF16),
        ),
        grid=(n // tm,),
        in_specs=[
            pl.BlockSpec((tm, D_MODEL), row),
            pl.BlockSpec((1, D_MODEL), fix),
            pl.BlockSpec((D_MODEL, IN_EXT), fix),
            pl.BlockSpec((1, Q_LORA), fix),
            pl.BlockSpec((Q_LORA, N_HEADS * QK_PAD), fix),
            pl.BlockSpec((1, KV_LORA), fix),
            pl.BlockSpec((KV_LORA, N_HEADS * NOPE_DIM), fix),
            pl.BlockSpec((N_HEADS * V_DIM, KV_LORA), fix),
            pl.BlockSpec((tm, 2 * ROPE_DIM), row),
            pl.BlockSpec((1, QK_PAD), fix),
            pl.BlockSpec((1, QK_PAD), fix),
        ],
        out_specs=(
            pl.BlockSpec((tm, POOL_WIDTH), row),
            pl.BlockSpec((tm, KV_LORA), row),
            pl.BlockSpec((tm, 2 * ROPE_DIM), row),
            pl.BlockSpec((N_HEADS, tm, QK_PAD), head),
            pl.BlockSpec((N_HEADS, tm, QK_PAD), head),
            pl.BlockSpec((N_HEADS, None, V_DIM, tm), lambda i: (0, i, 0, 0)),
        ),
        compiler_params=_params(("parallel",)),
        name="inproj",
    )(x, w["mix_norm"], w["w_in"], w["q_a_norm"], w["wq"], w["kv_a_norm"], w["wk"], w["wvt"],
      cs, w["gq"], w["gk"])


def _flash_kernel(q_ref, k_ref, v_ref, o_ref):
    i = pl.program_id(2)
    nh, bq, _ = q_ref.shape

    def scores(hd, j):
        off = pl.multiple_of(j * ATT_BLOCK, ATT_BLOCK)
        return _dot_nt(k_ref[hd, pl.ds(off, ATT_BLOCK), :], q_ref[hd])

    def update(hd, j, st, stats):
        m, l, acc = stats
        m_new = jnp.maximum(m, jnp.max(st, axis=0, keepdims=True))
        p = jnp.exp2(st - m_new)
        alpha = jnp.exp2(m - m_new)
        l = alpha * l + jnp.sum(p, axis=0, keepdims=True)
        acc = alpha * acc + _dot(v_ref[hd, j], p.astype(BF16))
        return m_new, l, acc

    def body(j, carry):
        return tuple((scores(hd, j + 1), update(hd, j, *carry[hd])) for hd in range(nh))

    init = (jnp.full((1, bq), -jnp.inf, F32), jnp.zeros((1, bq), F32), jnp.zeros((V_DIM, bq), F32))
    carry = lax.fori_loop(0, i, body, tuple((scores(hd, 0), init) for hd in range(nh)))
    key = lax.broadcasted_iota(jnp.int32, (ATT_BLOCK, bq), 0)
    qry = lax.broadcasted_iota(jnp.int32, (ATT_BLOCK, bq), 1)
    for hd in range(nh):
        st, stats = carry[hd]
        _, l, acc = update(hd, i, jnp.where(key <= qry, st, NEG_INF), stats)
        o_ref[hd] = acc / l


def _flash(q, k, v, *, batch, t_pad):
    nq = t_pad // ATT_BLOCK
    qmap = lambda b, h, i: (h, b * nq + i, 0)
    return pl.pallas_call(
        _flash_kernel,
        out_shape=jax.ShapeDtypeStruct(v.shape, F32),
        grid=(batch, N_HEADS // FLASH_HEADS, nq),
        in_specs=[
            pl.BlockSpec((FLASH_HEADS, ATT_BLOCK, QK_PAD), qmap),
            pl.BlockSpec((FLASH_HEADS, t_pad, QK_PAD), lambda b, h, i: (h, b, 0)),
            pl.BlockSpec((FLASH_HEADS, nq, V_DIM, ATT_BLOCK), lambda b, h, i: (h, b, 0, 0)),
        ],
        out_specs=pl.BlockSpec((FLASH_HEADS, None, V_DIM, ATT_BLOCK),
                               lambda b, h, i: (h, b * nq + i, 0, 0)),
        compiler_params=_params(("parallel", "parallel", "arbitrary")),
        name="flash",
    )(q, k, v)


def _merge_tail(x, d, a, pw_ref, ps_ref, pn_ref, an_ref, wo_ref):
    pool = jnp.concatenate(
        [_dot(d[:, g * POOL_GROUP_DIM:(g + 1) * POOL_GROUP_DIM].astype(BF16), pw_ref[g])
         for g in range(len(POOL_WINDOWS))], axis=-1) * ps_ref[...]
    cat = jnp.concatenate([_rms(pool, pn_ref[...]).astype(BF16),
                           _rms(a, an_ref[...]).astype(BF16)], axis=-1)
    return x + _dot(cat, wo_ref[...])


def _merge_prompt_kernel(x_ref, u_ref, halo_ref, a_ref, pw_ref, ps_ref, pn_ref, an_ref, wo_ref,
                         o_ref, ext_ref):
    i = pl.program_id(1)
    tm = u_ref.shape[0]
    ext_ref[0:HALO, :] = jnp.where(i > 0, halo_ref[...], 0.0)
    ext_ref[HALO:HALO + tm, :] = u_ref[...]
    pos = i * tm + lax.broadcasted_iota(jnp.int32, (tm, 1), 0)
    ds = []
    for g, w in enumerate(POOL_WINDOWS):
        sl = slice(g * POOL_GROUP_DIM, (g + 1) * POOL_GROUP_DIM)
        tok = ext_ref[HALO:HALO + tm, sl]
        acc = tok
        for k in range(1, w):
            acc = acc + ext_ref[HALO - k:HALO - k + tm, sl]
        cnt = jnp.minimum(pos + 1, w).astype(F32)
        ds.append(acc / cnt - tok)
    d = jnp.concatenate(ds, axis=-1)
    a = jnp.concatenate([a_ref[hd] for hd in range(N_HEADS)], axis=0).T
    o_ref[...] = _merge_tail(x_ref[...], d, a, pw_ref, ps_ref, pn_ref, an_ref, wo_ref)


def _merge_weight_specs(fix2, fix3):
    return [
        pl.BlockSpec((len(POOL_WINDOWS), POOL_GROUP_DIM, POOL_GROUP_DIM), fix3),
        pl.BlockSpec((1, POOL_WIDTH), fix2),
        pl.BlockSpec((1, POOL_WIDTH), fix2),
        pl.BlockSpec((1, ATTN_WIDTH), fix2),
        pl.BlockSpec((D_MODEL, D_MODEL), fix2),
    ]


def _merge_prompt(x, u, attn, w, *, batch, t_pad, tm):
    n = x.shape[0]
    nt = t_pad // tm
    row = lambda b, i: (b * nt + i, 0)
    return pl.pallas_call(
        _merge_prompt_kernel,
        out_shape=jax.ShapeDtypeStruct((n, D_MODEL), F32),
        grid=(batch, nt),
        in_specs=[
            pl.BlockSpec((tm, D_MODEL), row),
            pl.BlockSpec((tm, POOL_WIDTH), row),
            pl.BlockSpec((HALO, POOL_WIDTH),
                         lambda b, i: (jnp.maximum((b * nt + i) * (tm // HALO) - 1, 0), 0)),
            pl.BlockSpec((N_HEADS, None, V_DIM, tm), lambda b, i: (0, b * nt + i, 0, 0)),
        ] + _merge_weight_specs(lambda b, i: (0, 0), lambda b, i: (0, 0, 0)),
        out_specs=pl.BlockSpec((tm, D_MODEL), row),
        scratch_shapes=[pltpu.VMEM((HALO + tm, POOL_WIDTH), F32)],
        compiler_params=_params(("parallel", "arbitrary")),
        name="merge_prompt",
    )(x, u, u, attn, w["pool_w"], w["pool_scale"], w["pool_out_norm"], w["attn_out_norm"], w["w_out"])


def _merge_sample_kernel(x_ref, ue_ref, pc_ref, wv_ref, pw_ref, ps_ref, pn_ref, an_ref, wo_ref, o_ref):
    s = pl.program_id(0)
    ds = []
    for g, w in enumerate(POOL_WINDOWS):
        sl = slice(g * POOL_GROUP_DIM, (g + 1) * POOL_GROUP_DIM)
        tok = ue_ref[POOL_STATE + s, :, sl]
        acc = tok
        for k in range(1, w):
            acc = acc + ue_ref[POOL_STATE + s - k, :, sl]
        ds.append(acc * (1.0 / w) - tok)
    d = jnp.concatenate(ds, axis=-1)
    a = jnp.concatenate([_dot(pc_ref[hd].astype(BF16), wv_ref[hd]) for hd in range(N_HEADS)], axis=-1)
    o_ref[...] = _merge_tail(x_ref[...], d, a, pw_ref, ps_ref, pn_ref, an_ref, wo_ref)


def _merge_sample(x, ue, pc, w, *, dec_seq, dec_batch):
    n = x.shape[0]
    n_ext = ue.shape[0]
    return pl.pallas_call(
        _merge_sample_kernel,
        out_shape=jax.ShapeDtypeStruct((n, D_MODEL), F32),
        grid=(dec_seq,),
        in_specs=[
            pl.BlockSpec((dec_batch, D_MODEL), lambda s: (s, 0)),
            pl.BlockSpec((n_ext, dec_batch, POOL_WIDTH), lambda s: (0, 0, 0)),
            pl.BlockSpec((N_HEADS, dec_batch, KV_LORA), lambda s: (0, s, 0)),
            pl.BlockSpec((N_HEADS, KV_LORA, V_DIM), lambda s: (0, 0, 0)),
        ] + _merge_weight_specs(lambda s: (0, 0), lambda s: (0, 0, 0)),
        out_specs=pl.BlockSpec((dec_batch, D_MODEL), lambda s: (s, 0)),
        compiler_params=_params(("arbitrary",)),
        name="merge_sample",
    )(x, ue, pc, w["wv3"], w["pool_w"], w["pool_scale"], w["pool_out_norm"], w["attn_out_norm"], w["w_out"])


def _qabs_kernel(q_ref, k_ref, wk_ref, gk_ref, qa_ref, qr_ref, sn_ref, *, dec_seq):
    qb = q_ref[...]
    q = qb.astype(F32)
    gk = gk_ref[...]
    qn = (q[:, :NOPE_DIM] * gk[:, :NOPE_DIM]).astype(BF16)
    qa_ref[...] = _dot_nt(qn, wk_ref[...]).astype(BF16)
    qr_ref[...] = (q[:, NOPE_DIM:] * gk[:, NOPE_DIM:]).astype(BF16)
    nb = qb.shape[0] // dec_seq
    eye = (lax.broadcasted_iota(jnp.int32, (nb, nb), 0) == lax.broadcasted_iota(jnp.int32, (nb, nb), 1))
    for s in range(dec_seq):
        for j in range(dec_seq):
            r = s * dec_seq + j
            if j <= s:
                mm = _dot_nt(qb[s * nb:(s + 1) * nb], k_ref[j * nb:(j + 1) * nb, :])
                sn_ref[r:r + 1, :] = jnp.sum(jnp.where(eye, mm, 0.0), axis=0, keepdims=True)
            else:
                sn_ref[r:r + 1, :] = jnp.full((1, nb), NEG_INF, F32)


def _qabs(q, k, w, *, dec_seq):
    n = q.shape[1]
    head = lambda h: (h, 0, 0)
    return pl.pallas_call(
        functools.partial(_qabs_kernel, dec_seq=dec_seq),
        out_shape=(jax.ShapeDtypeStruct((N_HEADS, n, KV_LORA), BF16),
                   jax.ShapeDtypeStruct((N_HEADS, n, QK_PAD - NOPE_DIM), BF16),
                   jax.ShapeDtypeStruct((N_HEADS, dec_seq * dec_seq, n // dec_seq), F32)),
        grid=(N_HEADS,),
        in_specs=[
            pl.BlockSpec((None, n, QK_PAD), head),
            pl.BlockSpec((None, n, QK_PAD), head),
            pl.BlockSpec((None, KV_LORA, NOPE_DIM), head),
            pl.BlockSpec((1, QK_PAD), lambda h: (0, 0)),
        ],
        out_specs=(pl.BlockSpec((None, n, KV_LORA), head),
                   pl.BlockSpec((None, n, QK_PAD - NOPE_DIM), head),
                   pl.BlockSpec((None, dec_seq * dec_seq, n // dec_seq), head)),
        compiler_params=_params(("parallel",)),
        name="qabs",
    )(q, k, w["wk3"], w["gk"])


def _sattn_kernel(pt_ref, qa_ref, qr_ref, snew_ref, cnew_ref, wkt_ref, ckv_hbm, kpe_hbm, o_ref,
                  a_ref, cbuf, pbuf, cb16, s_all, sem, *, layer, n_pages, n_rows):
    b = pl.program_id(0)
    nb = pl.num_programs(0)
    slot = b % 2
    n_kn = N_HEADS * NOPE_DIM

    def page_copies(page, p, sl):
        return (pltpu.make_async_copy(ckv_hbm.at[layer, page], cbuf.at[sl, p], sem.at[0, sl]),
                pltpu.make_async_copy(kpe_hbm.at[layer, page], pbuf.at[sl, p], sem.at[1, sl]))

    def issue(seq, sl):
        def body(p, carry):
            for cp in page_copies(pt_ref[seq, p], p, sl):
                cp.start()
            return carry
        lax.fori_loop(0, n_pages, body, 0, unroll=8)

    def wait(sl):
        for p in range(n_pages):
            for cp in page_copies(0, p, sl):
                cp.wait()

    @pl.when(b == 0)
    def _():
        a_ref[0:n_kn, :] = wkt_ref[...]
        issue(0, 0)

    @pl.when(b + 1 < nb)
    def _():
        issue(b + 1, 1 - slot)

    a_ref[n_kn:n_kn + n_rows, :] = qa_ref[...]
    qr = qr_ref[...]
    n_chunks = n_pages * PAGE_SIZE // KEY_CHUNK
    pages_per_chunk = KEY_CHUNK // PAGE_SIZE

    def scores(cb, pt):
        nk = cb.shape[0]
        r_all = _dot_nt(a_ref[...], cb)
        kn = r_all[:n_kn]
        ssq = jnp.sum((kn * kn).reshape(N_HEADS, NOPE_DIM, nk), axis=1)
        pe2 = jnp.sum(pt * pt, axis=0, keepdims=True)
        r = lax.rsqrt((ssq + pe2) * (1.0 / QK_DIM) + EPS)
        raw_r = _dot(qr, pt.astype(BF16))
        return (r_all[n_kn:] + raw_r) * jnp.concatenate([r] * (n_rows // N_HEADS), axis=0)

    def score_group(g):
        for j in range(CHUNK_GROUP):
            c = g * CHUNK_GROUP + j
            cb = cbuf[slot, pl.ds(c * pages_per_chunk, pages_per_chunk)] \
                .reshape(KEY_CHUNK, KV_LORA).astype(BF16)
            cb16[pl.ds(pl.multiple_of(c * KEY_CHUNK, KEY_CHUNK), KEY_CHUNK), :] = cb
            pt = jnp.concatenate([pbuf[slot, c * pages_per_chunk + k] for k in range(pages_per_chunk)],
                                 axis=1)
            s_all[c] = scores(cb, pt)

    def accumulate(s, c_rows, carry):
        m, l, acc = carry
        m_new = jnp.maximum(m, jnp.max(s, axis=-1, keepdims=True))
        p = jnp.exp2(s - m_new)
        alpha = jnp.exp2(m - m_new)
        l = alpha * l + jnp.sum(p, axis=-1, keepdims=True)
        acc = alpha * acc + _dot(p.astype(BF16), c_rows)
        return m_new, l, acc

    def accumulate_group(g, carry):
        s = jnp.concatenate([s_all[g * CHUNK_GROUP + j] for j in range(CHUNK_GROUP)], axis=1)
        n_g = CHUNK_GROUP * KEY_CHUNK
        return accumulate(s, cb16[pl.ds(pl.multiple_of(g * n_g, n_g), n_g), :], carry)

    wait(slot)
    n_groups = n_chunks // CHUNK_GROUP
    score_group(0)

    def body(g, carry):
        carry = accumulate_group(g - 1, carry)
        score_group(g)
        return carry

    carry = (jnp.full((n_rows, 1), -jnp.inf, F32), jnp.zeros((n_rows, 1), F32),
             jnp.zeros((n_rows, KV_LORA), F32))
    carry = lax.fori_loop(1, n_groups, body, carry)
    carry = accumulate_group(n_groups - 1, carry)
    _, l, acc = accumulate(snew_ref[...], cnew_ref[...], carry)
    o_ref[...] = acc / l


def _sattn(page_table, qa, qr, snew, cnew, wkt, cache_ckv, cache_kpe_t, *, layer):
    nb, n_pages = page_table.shape
    n_rows = qa.shape[1]
    n_keys = n_pages * PAGE_SIZE
    n_chunks = n_keys // KEY_CHUNK
    assert n_chunks % CHUNK_GROUP == 0 and KEY_CHUNK % PAGE_SIZE == 0
    seq3 = lambda b, pt: (b, 0, 0)
    grid_spec = pltpu.PrefetchScalarGridSpec(
        num_scalar_prefetch=1,
        grid=(nb,),
        in_specs=[
            pl.BlockSpec((None, n_rows, KV_LORA), seq3),
            pl.BlockSpec((None, n_rows, ROPE_DIM), seq3),
            pl.BlockSpec((None, n_rows, PAGE_SIZE), seq3),
            pl.BlockSpec((None, PAGE_SIZE, KV_LORA), seq3),
            pl.BlockSpec((N_HEADS * NOPE_DIM, KV_LORA), lambda b, pt: (0, 0)),
            pl.BlockSpec(memory_space=pl.ANY),
            pl.BlockSpec(memory_space=pl.ANY),
        ],
        out_specs=pl.BlockSpec((None, n_rows, KV_LORA), seq3),
        scratch_shapes=[
            pltpu.VMEM((N_HEADS * NOPE_DIM + n_rows, KV_LORA), BF16),
            pltpu.VMEM((2, n_pages, PAGE_SIZE, KV_LORA), F32),
            pltpu.VMEM((2, n_pages, ROPE_DIM, PAGE_SIZE), F32),
            pltpu.VMEM((n_keys, KV_LORA), BF16),
            pltpu.VMEM((n_chunks, n_rows, KEY_CHUNK), F32),
            pltpu.SemaphoreType.DMA((2, 2)),
        ],
    )
    return pl.pallas_call(
        functools.partial(_sattn_kernel, layer=layer, n_pages=n_pages, n_rows=n_rows),
        out_shape=jax.ShapeDtypeStruct((nb, n_rows, KV_LORA), F32),
        grid_spec=grid_spec,
        compiler_params=_params(("arbitrary",)),
        name="sattn",
    )(page_table, qa, qr, snew, cnew, wkt, cache_ckv, cache_kpe_t)


def _rot_half_cols(w):
    return jnp.concatenate([-w[..., ROPE_HALF:], w[..., :ROPE_HALF]], axis=-1)


def _rope_table(pos):
    inv = ROPE_THETA ** (-jnp.arange(ROPE_HALF, dtype=F32) / ROPE_HALF)
    ang = pos.astype(F32)[:, None] * inv[None, :]
    c, s = jnp.cos(ang), jnp.sin(ang)
    return jnp.concatenate([c, c, s, s], axis=-1)


def _layer_weights(l, p):
    row = lambda v: v[l][None, :].astype(F32)
    w_in = p["w_in"][l]
    w_pe = w_in[:, POOL_WIDTH + Q_LORA + KV_LORA:]
    wq = p["w_q_b"][l].reshape(Q_LORA, N_HEADS, QK_DIM)
    wq_rope = wq[..., NOPE_DIM:]
    wkv = p["w_kv_b"][l].reshape(KV_LORA, N_HEADS, NOPE_DIM + V_DIM)
    wk = wkv[..., :NOPE_DIM]
    wv = wkv[..., NOPE_DIM:]
    zeros = jnp.zeros((QK_PAD - QK_DIM,), F32)

    def head_gain(g_nope, g_rope):
        return jnp.concatenate([g_nope[l], g_rope[l], g_rope[l], zeros])[None, :]

    out = {
        "mix_norm": row(p["mix_norm"]),
        "w_in": jnp.concatenate([w_in, _rot_half_cols(w_pe)], axis=-1).astype(BF16),
        "q_a_norm": row(p["q_a_norm"]),
        "wq": jnp.concatenate([wq, _rot_half_cols(wq_rope)], axis=-1)
              .reshape(Q_LORA, N_HEADS * QK_PAD).astype(BF16),
        "kv_a_norm": row(p["kv_a_norm"]),
        "wk": wk.reshape(KV_LORA, N_HEADS * NOPE_DIM).astype(BF16),
        "wvt": wv.reshape(KV_LORA, N_HEADS * V_DIM).T.astype(BF16),
        "wk3": wk.transpose(1, 0, 2).astype(BF16),
        "wkt": wk.reshape(KV_LORA, N_HEADS * NOPE_DIM).T.astype(BF16),
        "wv3": wv.transpose(1, 0, 2).astype(BF16),
        "gq": head_gain(p["q_norm_nope"], p["q_norm_rope"]) * (ATTN_SCALE * LOG2_E),
        "gk": head_gain(p["k_norm_nope"], p["k_norm_rope"]),
        "pool_w": p["pool_w"][l].astype(BF16),
        "pool_scale": row(p["pool_scale"]),
        "pool_out_norm": row(p["pool_out_norm"]),
        "attn_out_norm": row(p["attn_out_norm"]),
        "w_out": p["w_out"][l].astype(BF16),
        "ffn1_norm": row(p["ffn1_norm"]),
        "ffn2_norm": row(p["ffn2_norm"]),
    }
    return out


def kernel(x_prompt, x_sample, cache_ckv, cache_kpe, state_pool, page_table, meta_tokens, ffn1_norm, ffn1_w_gate, ffn1_w_up, ffn1_w_down, mix_norm, w_in, pool_w, pool_scale, q_a_norm, w_q_b, kv_a_norm, w_kv_b, q_norm_nope, q_norm_rope, k_norm_nope, k_norm_rope, pool_out_norm, attn_out_norm, w_out, ffn2_norm, ffn2_w_gate, ffn2_w_up, ffn2_w_down):
    p = dict(ffn1_norm=ffn1_norm, ffn1_w_gate=ffn1_w_gate, ffn1_w_up=ffn1_w_up, ffn1_w_down=ffn1_w_down,
             mix_norm=mix_norm, w_in=w_in, pool_w=pool_w, pool_scale=pool_scale, q_a_norm=q_a_norm,
             w_q_b=w_q_b, kv_a_norm=kv_a_norm, w_kv_b=w_kv_b, q_norm_nope=q_norm_nope,
             q_norm_rope=q_norm_rope, k_norm_nope=k_norm_nope, k_norm_rope=k_norm_rope,
             pool_out_norm=pool_out_norm, attn_out_norm=attn_out_norm, w_out=w_out,
             ffn2_norm=ffn2_norm, ffn2_w_gate=ffn2_w_gate, ffn2_w_up=ffn2_w_up, ffn2_w_down=ffn2_w_down)
    depth = w_in.shape[0]
    batch, seq, _ = x_prompt.shape
    dec_batch, dec_seq, _ = x_sample.shape
    n_pages = page_table.shape[1]
    t_real = N_META + seq
    t_pad = -(-t_real // ATT_BLOCK) * ATT_BLOCK
    n_p = batch * t_pad
    n_s = dec_seq * dec_batch
    assert n_p % FFN_ROWS == 0 and dec_batch % 8 == 0 and dec_seq * N_HEADS % 8 == 0
    ffn_w = {name: tuple(p[name + suffix].astype(BF16) for suffix in ("_w_gate", "_w_up", "_w_down"))
             for name in ("ffn1", "ffn2")}
    cache_kpe_t = jnp.swapaxes(cache_kpe, 2, 3)

    meta = jnp.broadcast_to(meta_tokens[None].astype(F32), (batch, N_META, D_MODEL))
    xp = jnp.concatenate([meta, x_prompt, jnp.zeros((batch, t_pad - t_real, D_MODEL), F32)], axis=1)
    xp = xp.reshape(n_p, D_MODEL)
    xs = x_sample.transpose(1, 0, 2).reshape(n_s, D_MODEL)
    cs_p = jnp.tile(_rope_table(jnp.arange(t_pad)), (batch, 1))
    cs_s = jnp.repeat(_rope_table(n_pages * PAGE_SIZE + jnp.arange(dec_seq)), dec_batch, axis=0)

    outs = [[] for _ in range(6)]
    for l in range(depth):
        w = _layer_weights(l, p)
        xp = _ffn(xp, w["ffn1_norm"], *ffn_w["ffn1"], layer=l, tm=FFN_ROWS)
        xs = _ffn(xs, w["ffn1_norm"], *ffn_w["ffn1"], layer=l, tm=n_s)

        u_p, ckv_p, kpe_p, q_p, k_p, v_p = _inproj(xp, cs_p, w)
        attn_p = _flash(q_p, k_p, v_p, batch=batch, t_pad=t_pad)
        xp = _merge_prompt(xp, u_p, attn_p, w, batch=batch, t_pad=t_pad, tm=ATT_BLOCK)
        outs[0].append(ckv_p.reshape(batch, t_pad, KV_LORA)[:, :t_real])
        outs[1].append(kpe_p.reshape(batch, t_pad, 2 * ROPE_DIM)[:, :t_real, :ROPE_DIM])
        outs[2].append(u_p.reshape(batch, t_pad, POOL_WIDTH)[:, t_real - POOL_STATE:t_real])

        u_s, ckv_s, kpe_s, q_s, k_s, _ = _inproj(xs, cs_s, w)
        qa, qr, sn = _qabs(q_s, k_s, w, dec_seq=dec_seq)

        def per_seq(t):
            return t.reshape(N_HEADS, dec_seq, dec_batch, -1).transpose(2, 1, 0, 3) \
                    .reshape(dec_batch, dec_seq * N_HEADS, -1)

        c_new = ckv_s.reshape(dec_seq, dec_batch, KV_LORA).transpose(1, 0, 2)
        p_new = kpe_s[:, :ROPE_DIM].reshape(dec_seq, dec_batch, ROPE_DIM).transpose(1, 0, 2)
        n_fill = PAGE_SIZE - dec_seq
        sn = sn.reshape(N_HEADS, dec_seq, dec_seq, dec_batch).transpose(3, 1, 0, 2) \
               .reshape(dec_batch, dec_seq * N_HEADS, dec_seq)
        pc = _sattn(page_table, per_seq(qa), per_seq(qr[..., :ROPE_DIM]),
                    jnp.pad(sn, ((0, 0), (0, 0), (0, n_fill)), constant_values=NEG_INF),
                    jnp.pad(c_new, ((0, 0), (0, n_fill), (0, 0))).astype(BF16),
                    w["wkt"], cache_ckv, cache_kpe_t, layer=l)
        pc = pc.reshape(dec_batch, dec_seq, N_HEADS, KV_LORA).transpose(2, 1, 0, 3) \
               .reshape(N_HEADS, n_s, KV_LORA)
        ue = jnp.concatenate([state_pool[l].astype(F32).transpose(1, 0, 2),
                              u_s.reshape(dec_seq, dec_batch, POOL_WIDTH)], axis=0)
        xs = _merge_sample(xs, ue, pc, w, dec_seq=dec_seq, dec_batch=dec_batch)
        outs[3].append(c_new)
        outs[4].append(p_new)
        outs[5].append(ue[-POOL_STATE:].transpose(1, 0, 2))

        xp = _ffn(xp, w["ffn2_norm"], *ffn_w["ffn2"], layer=l, tm=FFN_ROWS)
        xs = _ffn(xs, w["ffn2_norm"], *ffn_w["ffn2"], layer=l, tm=n_s)

    y_prompt = xp.reshape(batch, t_pad, D_MODEL)[:, N_META:t_real]
    y_sample = xs.reshape(dec_seq, dec_batch, D_MODEL).transpose(1, 0, 2)
    return (y_prompt, y_sample) + tuple(jnp.stack(o) for o in outs)
```

```python
import functools

import jax
import jax.numpy as jnp
from jax import lax
from jax.experimental import pallas as pl
from jax.experimental.pallas import tpu as pltpu

F32 = jnp.float32
BF16 = jnp.bfloat16

D_MODEL = 2048
N_META = 16
POOL_WIDTH = 1024
POOL_WINDOWS = (2, 4, 8, 16)
POOL_GROUP_DIM = 256
POOL_STATE = 15
N_HEADS = 8
NOPE_DIM = 128
ROPE_DIM = 64
ROPE_HALF = 32
QK_DIM = 192
QK_PAD = 256
V_DIM = 128
ATTN_WIDTH = 1024
Q_LORA = 512
KV_LORA = 256
D_FF = 5632
ROPE_THETA = 10000.0
EPS = 1e-6
ATTN_SCALE = QK_DIM ** -0.5
LOG2_E = 1.4426950408889634
NEG_INF = -1e30
PAGE_SIZE = 128

IN_EXT = POOL_WIDTH + Q_LORA + KV_LORA + 2 * ROPE_DIM
FF_TILE = 512
FFN_ROWS = 544
ATT_BLOCK = 256
FLASH_HEADS = 4
HALO = 16
KEY_CHUNK = 256
CHUNK_GROUP = 8
VMEM_LIMIT = 56 * 1024 * 1024


def _rms(x, g):
    return x * lax.rsqrt(jnp.mean(x * x, axis=-1, keepdims=True) + EPS) * g


def _dot(a, b):
    return jnp.dot(a, b, preferred_element_type=F32)


def _dot_nt(a, b):
    return lax.dot_general(a, b, (((1,), (1,)), ((), ())), preferred_element_type=F32)


def _params(sem, vmem=VMEM_LIMIT, flags=None):
    return pltpu.CompilerParams(dimension_semantics=sem, vmem_limit_bytes=vmem, flags=flags)


def _ffn_kernel(x_ref, g_ref, wg_ref, wu_ref, wd_ref, o_ref, h_ref):
    j = pl.program_id(1)

    @pl.when(j == 0)
    def _():
        x = x_ref[...]
        h_ref[...] = _rms(x, g_ref[...]).astype(BF16)
        o_ref[...] = x

    h = h_ref[...]
    gate = _dot(h, wg_ref[...])
    up = _dot(h, wu_ref[...])
    a = (0.5 * gate * jax.nn.sigmoid(gate) * up).astype(BF16)
    o_ref[...] += _dot(a, wd_ref[...])


def _ffn(x, g, wg, wu, wd, *, layer, tm):
    n = x.shape[0]
    return pl.pallas_call(
        _ffn_kernel,
        out_shape=jax.ShapeDtypeStruct((n, D_MODEL), F32),
        grid=(n // tm, D_FF // FF_TILE),
        in_specs=[
            pl.BlockSpec((tm, D_MODEL), lambda i, j: (i, 0)),
            pl.BlockSpec((1, D_MODEL), lambda i, j: (0, 0)),
            pl.BlockSpec((None, D_MODEL, FF_TILE), lambda i, j: (layer, 0, j)),
            pl.BlockSpec((None, D_MODEL, FF_TILE), lambda i, j: (layer, 0, j)),
            pl.BlockSpec((None, FF_TILE, D_MODEL), lambda i, j: (layer, j, 0)),
        ],
        out_specs=pl.BlockSpec((tm, D_MODEL), lambda i, j: (i, 0)),
        scratch_shapes=[pltpu.VMEM((tm, D_MODEL), BF16)],
        compiler_params=_params(("parallel", "arbitrary")),
        name="ffn",
    )(x, g, wg, wu, wd)


def _inproj_kernel(x_ref, gmix_ref, win_ref, gqa_ref, wq_ref, gkva_ref, wk_ref, wv_ref,
                   cs_ref, gq_ref, gk_ref,
                   u_ref, ckv_ref, kpe_ref, q_ref, k_ref, v_ref):
    h = _rms(x_ref[...], gmix_ref[...]).astype(BF16)
    z = _dot_nt(h, win_ref[...])
    u_ref[...] = z[:, :POOL_WIDTH]
    ql = _rms(z[:, POOL_WIDTH:POOL_WIDTH + Q_LORA], gqa_ref[...]).astype(BF16)
    o_kv = POOL_WIDTH + Q_LORA
    c = _rms(z[:, o_kv:o_kv + KV_LORA], gkva_ref[...])
    ckv_ref[...] = c

    cs = cs_ref[...]
    lane = lax.broadcasted_iota(jnp.int32, cs.shape, 1)
    low = lane < ROPE_DIM

    def rope(t):
        t = t * cs
        return jnp.where(low, t + pltpu.roll(t, ROPE_DIM, axis=1), 0.0)

    kpe = rope(z[:, o_kv + KV_LORA:])
    kpe_ref[...] = kpe
    kpe_ss = jnp.sum(kpe * kpe, axis=-1, keepdims=True)

    q = _dot(ql, wq_ref[...])
    cb = c.astype(BF16)
    kn = _dot(cb, wk_ref[...])
    vt = _dot_nt(wv_ref[...], cb)
    gq = gq_ref[...]
    gk = gk_ref[...]
    inv_d = 1.0 / QK_DIM
    for hd in range(N_HEADS):
        qa = q[:, hd * QK_PAD:hd * QK_PAD + NOPE_DIM]
        qb = rope(q[:, hd * QK_PAD + NOPE_DIM:(hd + 1) * QK_PAD])
        rs = lax.rsqrt(jnp.sum(qa * qa + qb * qb, axis=-1, keepdims=True) * inv_d + EPS)
        q_ref[hd, :, :NOPE_DIM] = (qa * rs * gq[:, :NOPE_DIM]).astype(BF16)
        q_ref[hd, :, NOPE_DIM:] = (qb * rs * gq[:, NOPE_DIM:]).astype(BF16)
        ka = kn[:, hd * NOPE_DIM:(hd + 1) * NOPE_DIM]
        rk = lax.rsqrt((jnp.sum(ka * ka, axis=-1, keepdims=True) + kpe_ss) * inv_d + EPS)
        k_ref[hd, :, :NOPE_DIM] = (ka * rk * gk[:, :NOPE_DIM]).astype(BF16)
        k_ref[hd, :, NOPE_DIM:] = (kpe * rk * gk[:, NOPE_DIM:]).astype(BF16)
        v_ref[hd] = vt[hd * V_DIM:(hd + 1) * V_DIM, :].astype(BF16)


def _inproj(x, cs, w_in_t, w, *, layer):
    n = x.shape[0]
    tm = ATT_BLOCK
    row = lambda i: (i, 0)
    fix = lambda i: (0, 0)
    head = lambda i: (0, i, 0)
    return pl.pallas_call(
        _inproj_kernel,
        out_shape=(
            jax.ShapeDtypeStruct((n, POOL_WIDTH), F32),
            jax.ShapeDtypeStruct((n, KV_LORA), F32),
            jax.ShapeDtypeStruct((n, 2 * ROPE_DIM), F32),
            jax.ShapeDtypeStruct((N_HEADS, n, QK_PAD), BF16),
            jax.ShapeDtypeStruct((N_HEADS, n, QK_PAD), BF16),
            jax.ShapeDtypeStruct((N_HEADS, n // tm, V_DIM, tm), BF16),
        ),
        grid=(n // tm,),
        in_specs=[
            pl.BlockSpec((tm, D_MODEL), row),
            pl.BlockSpec((1, D_MODEL), fix),
            pl.BlockSpec((None, IN_EXT, D_MODEL), lambda i: (layer, 0, 0)),
            pl.BlockSpec((1, Q_LORA), fix),
            pl.BlockSpec((Q_LORA, N_HEADS * QK_PAD), fix),
            pl.BlockSpec((1, KV_LORA), fix),
            pl.BlockSpec((KV_LORA, N_HEADS * NOPE_DIM), fix),
            pl.BlockSpec((N_HEADS * V_DIM, KV_LORA), fix),
            pl.BlockSpec((tm, 2 * ROPE_DIM), row),
            pl.BlockSpec((1, QK_PAD), fix),
            pl.BlockSpec((1, QK_PAD), fix),
        ],
        out_specs=(
            pl.BlockSpec((tm, POOL_WIDTH), row),
            pl.BlockSpec((tm, KV_LORA), row),
            pl.BlockSpec((tm, 2 * ROPE_DIM), row),
            pl.BlockSpec((N_HEADS, tm, QK_PAD), head),
            pl.BlockSpec((N_HEADS, tm, QK_PAD), head),
            pl.BlockSpec((N_HEADS, None, V_DIM, tm), lambda i: (0, i, 0, 0)),
        ),
        compiler_params=_params(("parallel",)),
        name="inproj",
    )(x, w["mix_norm"], w_in_t, w["q_a_norm"], w["wq"], w["kv_a_norm"], w["wk"], w["wvt"],
      cs, w["gq"], w["gk"])


def _flash_kernel(q_ref, k_ref, v_ref, o_ref):
    i = pl.program_id(2)
    nh, bq, _ = q_ref.shape

    def scores(hd, j):
        off = pl.multiple_of(j * ATT_BLOCK, ATT_BLOCK)
        return _dot_nt(k_ref[hd, pl.ds(off, ATT_BLOCK), :], q_ref[hd])

    def update(hd, j, st, stats):
        m, l, acc = stats
        m_new = jnp.maximum(m, jnp.max(st, axis=0, keepdims=True))
        p = jnp.exp2(st - m_new)
        alpha = jnp.exp2(m - m_new)
        l = alpha * l + jnp.sum(p, axis=0, keepdims=True)
        acc = alpha * acc + _dot(v_ref[hd, j], p.astype(BF16))
        return m_new, l, acc

    def body(j, carry):
        return tuple((scores(hd, j + 1), update(hd, j, *carry[hd])) for hd in range(nh))

    init = (jnp.full((1, bq), -jnp.inf, F32), jnp.zeros((1, bq), F32), jnp.zeros((V_DIM, bq), F32))
    carry = lax.fori_loop(0, i, body, tuple((scores(hd, 0), init) for hd in range(nh)))
    key = lax.broadcasted_iota(jnp.int32, (ATT_BLOCK, bq), 0)
    qry = lax.broadcasted_iota(jnp.int32, (ATT_BLOCK, bq), 1)
    for hd in range(nh):
        st, stats = carry[hd]
        _, l, acc = update(hd, i, jnp.where(key <= qry, st, NEG_INF), stats)
        o_ref[hd] = acc / l


def _flash(q, k, v, *, batch, t_pad):
    nq = t_pad // ATT_BLOCK
    qmap = lambda b, h, i: (h, b * nq + i, 0)
    return pl.pallas_call(
        _flash_kernel,
        out_shape=jax.ShapeDtypeStruct(v.shape, F32),
        grid=(batch, N_HEADS // FLASH_HEADS, nq),
        in_specs=[
            pl.BlockSpec((FLASH_HEADS, ATT_BLOCK, QK_PAD), qmap),
            pl.BlockSpec((FLASH_HEADS, t_pad, QK_PAD), lambda b, h, i: (h, b, 0)),
            pl.BlockSpec((FLASH_HEADS, nq, V_DIM, ATT_BLOCK), lambda b, h, i: (h, b, 0, 0)),
        ],
        out_specs=pl.BlockSpec((FLASH_HEADS, None, V_DIM, ATT_BLOCK),
                               lambda b, h, i: (h, b * nq + i, 0, 0)),
        compiler_params=_params(("parallel", "parallel", "arbitrary")),
        name="flash",
    )(q, k, v)


def _merge_tail(x, d, a, pw_ref, ps_ref, pn_ref, an_ref, wo_ref):
    pool = jnp.concatenate(
        [_dot(d[:, g * POOL_GROUP_DIM:(g + 1) * POOL_GROUP_DIM].astype(BF16), pw_ref[g])
         for g in range(len(POOL_WINDOWS))], axis=-1) * ps_ref[...]
    cat = jnp.concatenate([_rms(pool, pn_ref[...]).astype(BF16),
                           _rms(a, an_ref[...]).astype(BF16)], axis=-1)
    return x + _dot(cat, wo_ref[...])


def _merge_prompt_kernel(x_ref, u_ref, halo_ref, a_ref, pw_ref, ps_ref, pn_ref, an_ref, wo_ref,
                         o_ref, ext_ref):
    i = pl.program_id(1)
    tm = u_ref.shape[0]
    ext_ref[0:HALO, :] = jnp.where(i > 0, halo_ref[...], 0.0)
    ext_ref[HALO:HALO + tm, :] = u_ref[...]
    pos = i * tm + lax.broadcasted_iota(jnp.int32, (tm, 1), 0)
    ds = []
    for g, w in enumerate(POOL_WINDOWS):
        sl = slice(g * POOL_GROUP_DIM, (g + 1) * POOL_GROUP_DIM)
        tok = ext_ref[HALO:HALO + tm, sl]
        acc = tok
        for k in range(1, w):
            acc = acc + ext_ref[HALO - k:HALO - k + tm, sl]
        cnt = jnp.minimum(pos + 1, w).astype(F32)
        ds.append(acc / cnt - tok)
    d = jnp.concatenate(ds, axis=-1)
    a = jnp.concatenate([a_ref[hd] for hd in range(N_HEADS)], axis=0).T
    o_ref[...] = _merge_tail(x_ref[...], d, a, pw_ref, ps_ref, pn_ref, an_ref, wo_ref)


def _merge_weight_specs(fix2, fix3):
    return [
        pl.BlockSpec((len(POOL_WINDOWS), POOL_GROUP_DIM, POOL_GROUP_DIM), fix3),
        pl.BlockSpec((1, POOL_WIDTH), fix2),
        pl.BlockSpec((1, POOL_WIDTH), fix2),
        pl.BlockSpec((1, ATTN_WIDTH), fix2),
        pl.BlockSpec((D_MODEL, D_MODEL), fix2),
    ]


def _merge_prompt(x, u, attn, w, *, batch, t_pad, tm):
    n = x.shape[0]
    nt = t_pad // tm
    row = lambda b, i: (b * nt + i, 0)
    return pl.pallas_call(
        _merge_prompt_kernel,
        out_shape=jax.ShapeDtypeStruct((n, D_MODEL), F32),
        grid=(batch, nt),
        in_specs=[
            pl.BlockSpec((tm, D_MODEL), row),
            pl.BlockSpec((tm, POOL_WIDTH), row),
            pl.BlockSpec((HALO, POOL_WIDTH),
                         lambda b, i: (jnp.maximum((b * nt + i) * (tm // HALO) - 1, 0), 0)),
            pl.BlockSpec((N_HEADS, None, V_DIM, tm), lambda b, i: (0, b * nt + i, 0, 0)),
        ] + _merge_weight_specs(lambda b, i: (0, 0), lambda b, i: (0, 0, 0)),
        out_specs=pl.BlockSpec((tm, D_MODEL), row),
        scratch_shapes=[pltpu.VMEM((HALO + tm, POOL_WIDTH), F32)],
        compiler_params=_params(("parallel", "arbitrary")),
        name="merge_prompt",
    )(x, u, u, attn, w["pool_w"], w["pool_scale"], w["pool_out_norm"], w["attn_out_norm"], w["w_out"])


def _merge_sample_kernel(x_ref, st_ref, us_ref, pc_ref, wv_ref, pw_ref, ps_ref, pn_ref, an_ref, wo_ref,
                         o_ref):
    def ext_row(j, sl):
        return st_ref[j, :, sl] if j < POOL_STATE else us_ref[j - POOL_STATE, :, sl]

    rows = []
    for s in range(us_ref.shape[0]):
        ds = []
        for g, w in enumerate(POOL_WINDOWS):
            sl = slice(g * POOL_GROUP_DIM, (g + 1) * POOL_GROUP_DIM)
            tok = ext_row(POOL_STATE + s, sl)
            acc = tok
            for k in range(1, w):
                acc = acc + ext_row(POOL_STATE + s - k, sl)
            ds.append(acc * (1.0 / w) - tok)
        rows.append(jnp.concatenate(ds, axis=-1))
    d = jnp.concatenate(rows, axis=0)
    a = jnp.concatenate([_dot(pc_ref[hd].astype(BF16), wv_ref[hd]) for hd in range(N_HEADS)], axis=-1)
    o_ref[...] = _merge_tail(x_ref[...], d, a, pw_ref, ps_ref, pn_ref, an_ref, wo_ref)


def _merge_sample(x, state_t, us, pc, w, *, layer):
    n = x.shape[0]
    dec_seq, dec_batch, _ = us.shape
    once = pl.Buffered(1)
    return pl.pallas_call(
        _merge_sample_kernel,
        out_shape=jax.ShapeDtypeStruct((n, D_MODEL), F32),
        grid=(1,),
        in_specs=[
            pl.BlockSpec((n, D_MODEL), lambda s: (0, 0), pipeline_mode=once),
            pl.BlockSpec((None, POOL_STATE, dec_batch, POOL_WIDTH), lambda s: (layer, 0, 0, 0),
                         pipeline_mode=once),
            pl.BlockSpec((dec_seq, dec_batch, POOL_WIDTH), lambda s: (0, 0, 0), pipeline_mode=once),
            pl.BlockSpec((N_HEADS, n, KV_LORA), lambda s: (0, 0, 0), pipeline_mode=once),
            pl.BlockSpec((N_HEADS, KV_LORA, V_DIM), lambda s: (0, 0, 0), pipeline_mode=once),
            pl.BlockSpec((len(POOL_WINDOWS), POOL_GROUP_DIM, POOL_GROUP_DIM), lambda s: (0, 0, 0),
                         pipeline_mode=once),
            pl.BlockSpec((1, POOL_WIDTH), lambda s: (0, 0)),
            pl.BlockSpec((1, POOL_WIDTH), lambda s: (0, 0)),
            pl.BlockSpec((1, ATTN_WIDTH), lambda s: (0, 0)),
            pl.BlockSpec((D_MODEL, D_MODEL), lambda s: (0, 0), pipeline_mode=once),
        ],
        out_specs=pl.BlockSpec((n, D_MODEL), lambda s: (0, 0)),
        compiler_params=_params(("arbitrary",)),
        name="merge_sample",
    )(x, state_t, us, pc, w["wv3"], w["pool_w"], w["pool_scale"], w["pool_out_norm"],
      w["attn_out_norm"], w["w_out"])


def _qabs_kernel(q_ref, k_ref, wk_ref, gk_ref, qa_ref, qr_ref, sn_ref, *, dec_seq):
    qb = q_ref[...]
    q = qb.astype(F32)
    gk = gk_ref[...]
    qn = (q[:, :NOPE_DIM] * gk[:, :NOPE_DIM]).astype(BF16)
    qa_ref[...] = _dot_nt(qn, wk_ref[...]).astype(BF16)
    qr_ref[...] = (q[:, NOPE_DIM:] * gk[:, NOPE_DIM:]).astype(BF16)
    nb = qb.shape[0] // dec_seq
    eye = (lax.broadcasted_iota(jnp.int32, (nb, nb), 0) == lax.broadcasted_iota(jnp.int32, (nb, nb), 1))
    for s in range(dec_seq):
        for j in range(dec_seq):
            r = s * dec_seq + j
            if j <= s:
                mm = _dot_nt(qb[s * nb:(s + 1) * nb], k_ref[j * nb:(j + 1) * nb, :])
                sn_ref[r:r + 1, :] = jnp.sum(jnp.where(eye, mm, 0.0), axis=0, keepdims=True)
            else:
                sn_ref[r:r + 1, :] = jnp.full((1, nb), NEG_INF, F32)


def _qabs(q, k, w, *, dec_seq):
    n = q.shape[1]
    head = lambda h: (h, 0, 0)
    return pl.pallas_call(
        functools.partial(_qabs_kernel, dec_seq=dec_seq),
        out_shape=(jax.ShapeDtypeStruct((N_HEADS, n, KV_LORA), BF16),
                   jax.ShapeDtypeStruct((N_HEADS, n, QK_PAD - NOPE_DIM), BF16),
                   jax.ShapeDtypeStruct((N_HEADS, dec_seq * dec_seq, n // dec_seq), F32)),
        grid=(N_HEADS,),
        in_specs=[
            pl.BlockSpec((None, n, QK_PAD), head),
            pl.BlockSpec((None, n, QK_PAD), head),
            pl.BlockSpec((None, KV_LORA, NOPE_DIM), head),
            pl.BlockSpec((1, QK_PAD), lambda h: (0, 0)),
        ],
        out_specs=(pl.BlockSpec((None, n, KV_LORA), head),
                   pl.BlockSpec((None, n, QK_PAD - NOPE_DIM), head),
                   pl.BlockSpec((None, dec_seq * dec_seq, n // dec_seq), head)),
        compiler_params=_params(("parallel",)),
        name="qabs",
    )(q, k, w["wk3"], w["gk"])


def _sattn_kernel(pt_ref, qa_ref, qr_ref, snew_ref, cnew_ref, wkt_ref, ckv_hbm, kpe_hbm, o_ref,
                  a_ref, cbuf, pbuf, cb16, s_all, sem, *, layer, n_pages, n_rows):
    b = pl.program_id(0)
    nb = pl.num_programs(0)
    slot = b % 2
    n_kn = N_HEADS * NOPE_DIM

    def page_copies(page, p, sl):
        return (pltpu.make_async_copy(ckv_hbm.at[layer, page], cbuf.at[sl, p], sem.at[0, sl]),
                pltpu.make_async_copy(kpe_hbm.at[layer, page], pbuf.at[sl, p], sem.at[1, sl]))

    def issue(seq, sl):
        def body(p, carry):
            for cp in page_copies(pt_ref[seq, p], p, sl):
                cp.start()
            return carry
        lax.fori_loop(0, n_pages, body, 0, unroll=8)

    def wait(sl):
        for p in range(n_pages):
            for cp in page_copies(0, p, sl):
                cp.wait()

    @pl.when(b == 0)
    def _():
        a_ref[0:n_kn, :] = wkt_ref[...]
        issue(0, 0)

    @pl.when(b + 1 < nb)
    def _():
        issue(b + 1, 1 - slot)

    a_ref[n_kn:n_kn + n_rows, :] = qa_ref[...]
    qr = qr_ref[...]
    n_chunks = n_pages * PAGE_SIZE // KEY_CHUNK
    pages_per_chunk = KEY_CHUNK // PAGE_SIZE

    def scores(cb, pt):
        nk = cb.shape[0]
        r_all = _dot_nt(a_ref[...], cb)
        kn = r_all[:n_kn]
        ssq = jnp.sum((kn * kn).reshape(N_HEADS, NOPE_DIM, nk), axis=1)
        pe2 = jnp.sum(pt * pt, axis=0, keepdims=True)
        r = lax.rsqrt((ssq + pe2) * (1.0 / QK_DIM) + EPS)
        raw_r = _dot(qr, pt.astype(BF16))
        return (r_all[n_kn:] + raw_r) * jnp.concatenate([r] * (n_rows // N_HEADS), axis=0)

    def score_group(g):
        for j in range(CHUNK_GROUP):
            c = g * CHUNK_GROUP + j
            cb = cbuf[slot, pl.ds(c * pages_per_chunk, pages_per_chunk)] \
                .reshape(KEY_CHUNK, KV_LORA).astype(BF16)
            cb16[pl.ds(pl.multiple_of(c * KEY_CHUNK, KEY_CHUNK), KEY_CHUNK), :] = cb
            pt = jnp.concatenate([pbuf[slot, c * pages_per_chunk + k] for k in range(pages_per_chunk)],
                                 axis=1)
            s_all[c] = scores(cb, pt)

    def accumulate(s, c_rows, carry):
        m, l, acc = carry
        m_new = jnp.maximum(m, jnp.max(s, axis=-1, keepdims=True))
        p = jnp.exp2(s - m_new)
        alpha = jnp.exp2(m - m_new)
        l = alpha * l + jnp.sum(p, axis=-1, keepdims=True)
        acc = alpha * acc + _dot(p.astype(BF16), c_rows)
        return m_new, l, acc

    def accumulate_group(g, carry):
        s = jnp.concatenate([s_all[g * CHUNK_GROUP + j] for j in range(CHUNK_GROUP)], axis=1)
        n_g = CHUNK_GROUP * KEY_CHUNK
        return accumulate(s, cb16[pl.ds(pl.multiple_of(g * n_g, n_g), n_g), :], carry)

    wait(slot)
    n_groups = n_chunks // CHUNK_GROUP
    score_group(0)

    def body(g, carry):
        carry = accumulate_group(g - 1, carry)
        score_group(g)
        return carry

    carry = (jnp.full((n_rows, 1), -jnp.inf, F32), jnp.zeros((n_rows, 1), F32),
             jnp.zeros((n_rows, KV_LORA), F32))
    carry = lax.fori_loop(1, n_groups, body, carry)
    carry = accumulate_group(n_groups - 1, carry)
    _, l, acc = accumulate(snew_ref[...], cnew_ref[...], carry)
    o_ref[...] = acc / l


def _sattn(page_table, qa, qr, snew, cnew, wkt, cache_ckv, cache_kpe_t, *, layer):
    nb, n_pages = page_table.shape
    n_rows = qa.shape[1]
    n_keys = n_pages * PAGE_SIZE
    n_chunks = n_keys // KEY_CHUNK
    assert n_chunks % CHUNK_GROUP == 0 and KEY_CHUNK % PAGE_SIZE == 0
    seq3 = lambda b, pt: (b, 0, 0)
    grid_spec = pltpu.PrefetchScalarGridSpec(
        num_scalar_prefetch=1,
        grid=(nb,),
        in_specs=[
            pl.BlockSpec((None, n_rows, KV_LORA), seq3),
            pl.BlockSpec((None, n_rows, ROPE_DIM), seq3),
            pl.BlockSpec((None, n_rows, PAGE_SIZE), seq3),
            pl.BlockSpec((None, PAGE_SIZE, KV_LORA), seq3),
            pl.BlockSpec((N_HEADS * NOPE_DIM, KV_LORA), lambda b, pt: (0, 0)),
            pl.BlockSpec(memory_space=pl.ANY),
            pl.BlockSpec(memory_space=pl.ANY),
        ],
        out_specs=pl.BlockSpec((None, n_rows, KV_LORA), seq3),
        scratch_shapes=[
            pltpu.VMEM((N_HEADS * NOPE_DIM + n_rows, KV_LORA), BF16),
            pltpu.VMEM((2, n_pages, PAGE_SIZE, KV_LORA), F32),
            pltpu.VMEM((2, n_pages, ROPE_DIM, PAGE_SIZE), F32),
            pltpu.VMEM((n_keys, KV_LORA), BF16),
            pltpu.VMEM((n_chunks, n_rows, KEY_CHUNK), F32),
            pltpu.SemaphoreType.DMA((2, 2)),
        ],
    )
    return pl.pallas_call(
        functools.partial(_sattn_kernel, layer=layer, n_pages=n_pages, n_rows=n_rows),
        out_shape=jax.ShapeDtypeStruct((nb, n_rows, KV_LORA), F32),
        grid_spec=grid_spec,
        compiler_params=_params(("arbitrary",)),
        name="sattn",
    )(page_table, qa, qr, snew, cnew, wkt, cache_ckv, cache_kpe_t)


def _rot_half_cols(w):
    return jnp.concatenate([-w[..., ROPE_HALF:], w[..., :ROPE_HALF]], axis=-1)


def _rope_table(pos):
    inv = ROPE_THETA ** (-jnp.arange(ROPE_HALF, dtype=F32) / ROPE_HALF)
    ang = pos.astype(F32)[:, None] * inv[None, :]
    c, s = jnp.cos(ang), jnp.sin(ang)
    return jnp.concatenate([c, c, s, s], axis=-1)


def _layer_weights(l, p):
    row = lambda v: v[l][None, :].astype(F32)
    wq = p["w_q_b"][l].reshape(Q_LORA, N_HEADS, QK_DIM)
    wq_rope = wq[..., NOPE_DIM:]
    wkv = p["w_kv_b"][l].reshape(KV_LORA, N_HEADS, NOPE_DIM + V_DIM)
    wk = wkv[..., :NOPE_DIM]
    wv = wkv[..., NOPE_DIM:]
    zeros = jnp.zeros((QK_PAD - QK_DIM,), F32)

    def head_gain(g_nope, g_rope):
        return jnp.concatenate([g_nope[l], g_rope[l], g_rope[l], zeros])[None, :]

    out = {
        "mix_norm": row(p["mix_norm"]),
        "q_a_norm": row(p["q_a_norm"]),
        "wq": jnp.concatenate([wq, _rot_half_cols(wq_rope)], axis=-1)
              .reshape(Q_LORA, N_HEADS * QK_PAD).astype(BF16),
        "kv_a_norm": row(p["kv_a_norm"]),
        "wk": wk.reshape(KV_LORA, N_HEADS * NOPE_DIM).astype(BF16),
        "wvt": wv.reshape(KV_LORA, N_HEADS * V_DIM).T.astype(BF16),
        "wk3": wk.transpose(1, 0, 2).astype(BF16),
        "wkt": wk.reshape(KV_LORA, N_HEADS * NOPE_DIM).T.astype(BF16),
        "wv3": wv.transpose(1, 0, 2).astype(BF16),
        "gq": head_gain(p["q_norm_nope"], p["q_norm_rope"]) * (ATTN_SCALE * LOG2_E),
        "gk": head_gain(p["k_norm_nope"], p["k_norm_rope"]),
        "pool_w": p["pool_w"][l].astype(BF16),
        "pool_scale": row(p["pool_scale"]),
        "pool_out_norm": row(p["pool_out_norm"]),
        "attn_out_norm": row(p["attn_out_norm"]),
        "w_out": p["w_out"][l].astype(BF16),
        "ffn1_norm": row(p["ffn1_norm"]),
        "ffn2_norm": row(p["ffn2_norm"]),
    }
    return out


def kernel(x_prompt, x_sample, cache_ckv, cache_kpe, state_pool, page_table, meta_tokens, ffn1_norm, ffn1_w_gate, ffn1_w_up, ffn1_w_down, mix_norm, w_in, pool_w, pool_scale, q_a_norm, w_q_b, kv_a_norm, w_kv_b, q_norm_nope, q_norm_rope, k_norm_nope, k_norm_rope, pool_out_norm, attn_out_norm, w_out, ffn2_norm, ffn2_w_gate, ffn2_w_up, ffn2_w_down):
    p = dict(ffn1_norm=ffn1_norm, ffn1_w_gate=ffn1_w_gate, ffn1_w_up=ffn1_w_up, ffn1_w_down=ffn1_w_down,
             mix_norm=mix_norm, w_in=w_in, pool_w=pool_w, pool_scale=pool_scale, q_a_norm=q_a_norm,
             w_q_b=w_q_b, kv_a_norm=kv_a_norm, w_kv_b=w_kv_b, q_norm_nope=q_norm_nope,
             q_norm_rope=q_norm_rope, k_norm_nope=k_norm_nope, k_norm_rope=k_norm_rope,
             pool_out_norm=pool_out_norm, attn_out_norm=attn_out_norm, w_out=w_out,
             ffn2_norm=ffn2_norm, ffn2_w_gate=ffn2_w_gate, ffn2_w_up=ffn2_w_up, ffn2_w_down=ffn2_w_down)
    depth = w_in.shape[0]
    batch, seq, _ = x_prompt.shape
    dec_batch, dec_seq, _ = x_sample.shape
    n_pages = page_table.shape[1]
    t_real = N_META + seq
    t_pad = -(-t_real // ATT_BLOCK) * ATT_BLOCK
    n_p = batch * t_pad
    n_s = dec_seq * dec_batch
    assert n_p % FFN_ROWS == 0 and dec_batch % 8 == 0 and dec_seq * N_HEADS % 8 == 0
    ffn_w = {name: tuple(p[name + suffix].astype(BF16) for suffix in ("_w_gate", "_w_up", "_w_down"))
             for name in ("ffn1", "ffn2")}
    cache_kpe_t = jnp.swapaxes(cache_kpe, 2, 3)
    state_t = jnp.swapaxes(state_pool.astype(F32), 1, 2)
    w_in_t = jnp.swapaxes(w_in, 1, 2).astype(BF16)
    w_pe_t = w_in_t[:, POOL_WIDTH + Q_LORA + KV_LORA:]
    w_in_t = jnp.concatenate([w_in_t, -w_pe_t[:, ROPE_HALF:], w_pe_t[:, :ROPE_HALF]], axis=1)

    meta = jnp.broadcast_to(meta_tokens[None].astype(F32), (batch, N_META, D_MODEL))
    xp = jnp.concatenate([meta, x_prompt, jnp.zeros((batch, t_pad - t_real, D_MODEL), F32)], axis=1)
    xp = xp.reshape(n_p, D_MODEL)
    xs = x_sample.transpose(1, 0, 2).reshape(n_s, D_MODEL)
    cs_p = jnp.tile(_rope_table(jnp.arange(t_pad)), (batch, 1))
    cs_s = jnp.repeat(_rope_table(n_pages * PAGE_SIZE + jnp.arange(dec_seq)), dec_batch, axis=0)

    outs = [[] for _ in range(6)]
    for l in range(depth):
        w = _layer_weights(l, p)
        xp = _ffn(xp, w["ffn1_norm"], *ffn_w["ffn1"], layer=l, tm=FFN_ROWS)
        xs = _ffn(xs, w["ffn1_norm"], *ffn_w["ffn1"], layer=l, tm=n_s)

        u_p, ckv_p, kpe_p, q_p, k_p, v_p = _inproj(xp, cs_p, w_in_t, w, layer=l)
        attn_p = _flash(q_p, k_p, v_p, batch=batch, t_pad=t_pad)
        xp = _merge_prompt(xp, u_p, attn_p, w, batch=batch, t_pad=t_pad, tm=ATT_BLOCK)
        outs[0].append(ckv_p.reshape(batch, t_pad, KV_LORA)[:, :t_real])
        outs[1].append(kpe_p.reshape(batch, t_pad, 2 * ROPE_DIM)[:, :t_real, :ROPE_DIM])
        outs[2].append(u_p.reshape(batch, t_pad, POOL_WIDTH)[:, t_real - POOL_STATE:t_real])

        u_s, ckv_s, kpe_s, q_s, k_s, _ = _inproj(xs, cs_s, w_in_t, w, layer=l)
        qa, qr, sn = _qabs(q_s, k_s, w, dec_seq=dec_seq)

        def per_seq(t):
            return t.reshape(N_HEADS, dec_seq, dec_batch, -1).transpose(2, 1, 0, 3) \
                    .reshape(dec_batch, dec_seq * N_HEADS, -1)

        c_new = ckv_s.reshape(dec_seq, dec_batch, KV_LORA).transpose(1, 0, 2)
        p_new = kpe_s[:, :ROPE_DIM].reshape(dec_seq, dec_batch, ROPE_DIM).transpose(1, 0, 2)
        n_fill = PAGE_SIZE - dec_seq
        sn = sn.reshape(N_HEADS, dec_seq, dec_seq, dec_batch).transpose(3, 1, 0, 2) \
               .reshape(dec_batch, dec_seq * N_HEADS, dec_seq)
        pc = _sattn(page_table, per_seq(qa), per_seq(qr[..., :ROPE_DIM]),
                    jnp.pad(sn, ((0, 0), (0, 0), (0, n_fill)), constant_values=NEG_INF),
                    jnp.pad(c_new, ((0, 0), (0, n_fill), (0, 0))).astype(BF16),
                    w["wkt"], cache_ckv, cache_kpe_t, layer=l)
        pc = pc.reshape(dec_batch, dec_seq, N_HEADS, KV_LORA).transpose(2, 1, 0, 3) \
               .reshape(N_HEADS, n_s, KV_LORA)
        us = u_s.reshape(dec_seq, dec_batch, POOL_WIDTH)
        xs = _merge_sample(xs, state_t, us, pc, w, layer=l)
        outs[3].append(c_new)
        outs[4].append(p_new)
        outs[5].append(jnp.concatenate([state_t[l, dec_seq:], us], axis=0).transpose(1, 0, 2))

        xp = _ffn(xp, w["ffn2_norm"], *ffn_w["ffn2"], layer=l, tm=FFN_ROWS)
        xs = _ffn(xs, w["ffn2_norm"], *ffn_w["ffn2"], layer=l, tm=n_s)

    y_prompt = xp.reshape(batch, t_pad, D_MODEL)[:, N_META:t_real]
    y_sample = xs.reshape(dec_seq, dec_batch, D_MODEL).transpose(1, 0, 2)
    return (y_prompt, y_sample) + tuple(jnp.stack(o) for o in outs)
```

```python
import functools

import jax
import jax.numpy as jnp
from jax import lax
from jax.experimental import pallas as pl
from jax.experimental.pallas import tpu as pltpu

F32 = jnp.float32
BF16 = jnp.bfloat16

D_MODEL = 2048
N_META = 16
POOL_WIDTH = 1024
POOL_WINDOWS = (2, 4, 8, 16)
POOL_GROUP_DIM = 256
POOL_STATE = 15
N_HEADS = 8
NOPE_DIM = 128
ROPE_DIM = 64
ROPE_HALF = 32
QK_DIM = 192
QK_PAD = 256
V_DIM = 128
ATTN_WIDTH = 1024
Q_LORA = 512
KV_LORA = 256
D_FF = 5632
ROPE_THETA = 10000.0
EPS = 1e-6
ATTN_SCALE = QK_DIM ** -0.5
LOG2_E = 1.4426950408889634
NEG_INF = -1e30
PAGE_SIZE = 128

IN_EXT = POOL_WIDTH + Q_LORA + KV_LORA + 2 * ROPE_DIM
FF_TILE = 512
FF_CAST_TILE = 256
FFN_ROWS = 544
ATT_BLOCK = 256
FLASH_HEADS = 4
HALO = 16
KEY_CHUNK = 256
CHUNK_GROUP = 8
VMEM_LIMIT = 56 * 1024 * 1024


def _rms(x, g):
    return x * lax.rsqrt(jnp.mean(x * x, axis=-1, keepdims=True) + EPS) * g


def _dot(a, b):
    return jnp.dot(a, b, preferred_element_type=F32)


def _dot_nt(a, b):
    return lax.dot_general(a, b, (((1,), (1,)), ((), ())), preferred_element_type=F32)


def _params(sem, vmem=VMEM_LIMIT, flags=None):
    return pltpu.CompilerParams(dimension_semantics=sem, vmem_limit_bytes=vmem, flags=flags)


def _ffn_step(x_ref, g_ref, wg, wu, wd, o_ref, h_ref):
    @pl.when(pl.program_id(1) == 0)
    def _():
        x = x_ref[...]
        h_ref[...] = _rms(x, g_ref[...]).astype(BF16)
        o_ref[...] = x

    h = h_ref[...]
    gate = _dot(h, wg)
    up = _dot(h, wu)
    a = (0.5 * gate * jax.nn.sigmoid(gate) * up).astype(BF16)
    o_ref[...] += _dot(a, wd)


def _ffn_kernel(x_ref, g_ref, wg_ref, wu_ref, wd_ref, o_ref, h_ref):
    _ffn_step(x_ref, g_ref, wg_ref[...], wu_ref[...], wd_ref[...], o_ref, h_ref)


def _ffn_cast_kernel(x_ref, g_ref, wg_ref, wu_ref, wd_ref, o_ref, wg16_ref, wu16_ref, wd16_ref, h_ref):
    wg = wg_ref[...].astype(BF16)
    wu = wu_ref[...].astype(BF16)
    wd = wd_ref[...].astype(BF16)
    wg16_ref[...] = wg
    wu16_ref[...] = wu
    wd16_ref[...] = wd
    _ffn_step(x_ref, g_ref, wg, wu, wd, o_ref, h_ref)


def _ffn(x, g, wg, wu, wd, *, tm):
    n = x.shape[0]
    return pl.pallas_call(
        _ffn_kernel,
        out_shape=jax.ShapeDtypeStruct((n, D_MODEL), F32),
        grid=(n // tm, D_FF // FF_TILE),
        in_specs=[
            pl.BlockSpec((tm, D_MODEL), lambda i, j: (i, 0)),
            pl.BlockSpec((1, D_MODEL), lambda i, j: (0, 0)),
            pl.BlockSpec((D_MODEL, FF_TILE), lambda i, j: (0, j)),
            pl.BlockSpec((D_MODEL, FF_TILE), lambda i, j: (0, j)),
            pl.BlockSpec((FF_TILE, D_MODEL), lambda i, j: (j, 0)),
        ],
        out_specs=pl.BlockSpec((tm, D_MODEL), lambda i, j: (i, 0)),
        scratch_shapes=[pltpu.VMEM((tm, D_MODEL), BF16)],
        compiler_params=_params(("parallel", "arbitrary")),
        name="ffn",
    )(x, g, wg, wu, wd)


def _ffn_cast(x, g, wg, wu, wd, *, layer):
    n = x.shape[0]
    tf = FF_CAST_TILE
    w_in_spec = pl.BlockSpec((None, D_MODEL, tf), lambda i, j: (layer, 0, j))
    w_out_spec = pl.BlockSpec((D_MODEL, tf), lambda i, j: (0, j))
    return pl.pallas_call(
        _ffn_cast_kernel,
        out_shape=(jax.ShapeDtypeStruct((n, D_MODEL), F32),
                   jax.ShapeDtypeStruct((D_MODEL, D_FF), BF16),
                   jax.ShapeDtypeStruct((D_MODEL, D_FF), BF16),
                   jax.ShapeDtypeStruct((D_FF, D_MODEL), BF16)),
        grid=(1, D_FF // tf),
        in_specs=[
            pl.BlockSpec((n, D_MODEL), lambda i, j: (0, 0)),
            pl.BlockSpec((1, D_MODEL), lambda i, j: (0, 0)),
            w_in_spec,
            w_in_spec,
            pl.BlockSpec((None, tf, D_MODEL), lambda i, j: (layer, j, 0)),
        ],
        out_specs=(pl.BlockSpec((n, D_MODEL), lambda i, j: (0, 0)),
                   w_out_spec,
                   w_out_spec,
                   pl.BlockSpec((tf, D_MODEL), lambda i, j: (j, 0))),
        scratch_shapes=[pltpu.VMEM((n, D_MODEL), BF16)],
        compiler_params=_params(("arbitrary", "arbitrary")),
        name="ffn_cast",
    )(x, g, wg, wu, wd)


def _inproj_kernel(x_ref, gmix_ref, win_ref, gqa_ref, wq_ref, gkva_ref, wk_ref, wv_ref,
                   cs_ref, gq_ref, gk_ref,
                   u_ref, ckv_ref, kpe_ref, q_ref, k_ref, v_ref):
    h = _rms(x_ref[...], gmix_ref[...]).astype(BF16)
    z = _dot_nt(h, win_ref[...])
    u_ref[...] = z[:, :POOL_WIDTH]
    ql = _rms(z[:, POOL_WIDTH:POOL_WIDTH + Q_LORA], gqa_ref[...]).astype(BF16)
    o_kv = POOL_WIDTH + Q_LORA
    c = _rms(z[:, o_kv:o_kv + KV_LORA], gkva_ref[...])
    ckv_ref[...] = c

    cs = cs_ref[...]
    lane = lax.broadcasted_iota(jnp.int32, cs.shape, 1)
    low = lane < ROPE_DIM

    def rope(t):
        t = t * cs
        return jnp.where(low, t + pltpu.roll(t, ROPE_DIM, axis=1), 0.0)

    kpe = rope(z[:, o_kv + KV_LORA:])
    kpe_ref[...] = kpe
    kpe_ss = jnp.sum(kpe * kpe, axis=-1, keepdims=True)

    q = _dot(ql, wq_ref[...])
    cb = c.astype(BF16)
    kn = _dot(cb, wk_ref[...])
    vt = _dot_nt(wv_ref[...], cb)
    gq = gq_ref[...]
    gk = gk_ref[...]
    inv_d = 1.0 / QK_DIM
    for hd in range(N_HEADS):
        qa = q[:, hd * QK_PAD:hd * QK_PAD + NOPE_DIM]
        qb = rope(q[:, hd * QK_PAD + NOPE_DIM:(hd + 1) * QK_PAD])
        rs = lax.rsqrt(jnp.sum(qa * qa + qb * qb, axis=-1, keepdims=True) * inv_d + EPS)
        q_ref[hd, :, :NOPE_DIM] = (qa * rs * gq[:, :NOPE_DIM]).astype(BF16)
        q_ref[hd, :, NOPE_DIM:] = (qb * rs * gq[:, NOPE_DIM:]).astype(BF16)
        ka = kn[:, hd * NOPE_DIM:(hd + 1) * NOPE_DIM]
        rk = lax.rsqrt((jnp.sum(ka * ka, axis=-1, keepdims=True) + kpe_ss) * inv_d + EPS)
        k_ref[hd, :, :NOPE_DIM] = (ka * rk * gk[:, :NOPE_DIM]).astype(BF16)
        k_ref[hd, :, NOPE_DIM:] = (kpe * rk * gk[:, NOPE_DIM:]).astype(BF16)
        v_ref[hd] = vt[hd * V_DIM:(hd + 1) * V_DIM, :].astype(BF16)


def _inproj(x, cs, w_in_t, w, *, layer):
    n = x.shape[0]
    tm = ATT_BLOCK
    row = lambda i: (i, 0)
    fix = lambda i: (0, 0)
    head = lambda i: (0, i, 0)
    return pl.pallas_call(
        _inproj_kernel,
        out_shape=(
            jax.ShapeDtypeStruct((n, POOL_WIDTH), F32),
            jax.ShapeDtypeStruct((n, KV_LORA), F32),
            jax.ShapeDtypeStruct((n, 2 * ROPE_DIM), F32),
            jax.ShapeDtypeStruct((N_HEADS, n, QK_PAD), BF16),
            jax.ShapeDtypeStruct((N_HEADS, n, QK_PAD), BF16),
            jax.ShapeDtypeStruct((N_HEADS, n // tm, V_DIM, tm), BF16),
        ),
        grid=(n // tm,),
        in_specs=[
            pl.BlockSpec((tm, D_MODEL), row),
            pl.BlockSpec((1, D_MODEL), fix),
            pl.BlockSpec((None, IN_EXT, D_MODEL), lambda i: (layer, 0, 0)),
            pl.BlockSpec((1, Q_LORA), fix),
            pl.BlockSpec((Q_LORA, N_HEADS * QK_PAD), fix),
            pl.BlockSpec((1, KV_LORA), fix),
            pl.BlockSpec((KV_LORA, N_HEADS * NOPE_DIM), fix),
            pl.BlockSpec((N_HEADS * V_DIM, KV_LORA), fix),
            pl.BlockSpec((tm, 2 * ROPE_DIM), row),
            pl.BlockSpec((1, QK_PAD), fix),
            pl.BlockSpec((1, QK_PAD), fix),
        ],
        out_specs=(
            pl.BlockSpec((tm, POOL_WIDTH), row),
            pl.BlockSpec((tm, KV_LORA), row),
            pl.BlockSpec((tm, 2 * ROPE_DIM), row),
            pl.BlockSpec((N_HEADS, tm, QK_PAD), head),
            pl.BlockSpec((N_HEADS, tm, QK_PAD), head),
            pl.BlockSpec((N_HEADS, None, V_DIM, tm), lambda i: (0, i, 0, 0)),
        ),
        compiler_params=_params(("parallel",)),
        name="inproj",
    )(x, w["mix_norm"], w_in_t, w["q_a_norm"], w["wq"], w["kv_a_norm"], w["wk"], w["wvt"],
      cs, w["gq"], w["gk"])


def _flash_kernel(q_ref, k_ref, v_ref, o_ref):
    i = pl.program_id(2)
    nh, bq, _ = q_ref.shape

    def scores(hd, j):
        off = pl.multiple_of(j * ATT_BLOCK, ATT_BLOCK)
        return _dot_nt(k_ref[hd, pl.ds(off, ATT_BLOCK), :], q_ref[hd])

    def update(hd, j, st, stats):
        m, l, acc = stats
        m_new = jnp.maximum(m, jnp.max(st, axis=0, keepdims=True))
        p = jnp.exp2(st - m_new)
        alpha = jnp.exp2(m - m_new)
        l = alpha * l + jnp.sum(p, axis=0, keepdims=True)
        acc = alpha * acc + _dot(v_ref[hd, j], p.astype(BF16))
        return m_new, l, acc

    def body(j, carry):
        return tuple((scores(hd, j + 1), update(hd, j, *carry[hd])) for hd in range(nh))

    init = (jnp.full((1, bq), -jnp.inf, F32), jnp.zeros((1, bq), F32), jnp.zeros((V_DIM, bq), F32))
    carry = lax.fori_loop(0, i, body, tuple((scores(hd, 0), init) for hd in range(nh)))
    key = lax.broadcasted_iota(jnp.int32, (ATT_BLOCK, bq), 0)
    qry = lax.broadcasted_iota(jnp.int32, (ATT_BLOCK, bq), 1)
    for hd in range(nh):
        st, stats = carry[hd]
        _, l, acc = update(hd, i, jnp.where(key <= qry, st, NEG_INF), stats)
        o_ref[hd] = acc / l


def _flash(q, k, v, *, batch, t_pad):
    nq = t_pad // ATT_BLOCK
    qmap = lambda b, h, i: (h, b * nq + i, 0)
    return pl.pallas_call(
        _flash_kernel,
        out_shape=jax.ShapeDtypeStruct(v.shape, F32),
        grid=(batch, N_HEADS // FLASH_HEADS, nq),
        in_specs=[
            pl.BlockSpec((FLASH_HEADS, ATT_BLOCK, QK_PAD), qmap),
            pl.BlockSpec((FLASH_HEADS, t_pad, QK_PAD), lambda b, h, i: (h, b, 0)),
            pl.BlockSpec((FLASH_HEADS, nq, V_DIM, ATT_BLOCK), lambda b, h, i: (h, b, 0, 0)),
        ],
        out_specs=pl.BlockSpec((FLASH_HEADS, None, V_DIM, ATT_BLOCK),
                               lambda b, h, i: (h, b * nq + i, 0, 0)),
        compiler_params=_params(("parallel", "parallel", "arbitrary")),
        name="flash",
    )(q, k, v)


def _merge_tail(x, d, a, pw_ref, ps_ref, pn_ref, an_ref, wo_ref):
    pool = jnp.concatenate(
        [_dot(d[:, g * POOL_GROUP_DIM:(g + 1) * POOL_GROUP_DIM].astype(BF16), pw_ref[g])
         for g in range(len(POOL_WINDOWS))], axis=-1) * ps_ref[...]
    cat = jnp.concatenate([_rms(pool, pn_ref[...]).astype(BF16),
                           _rms(a, an_ref[...]).astype(BF16)], axis=-1)
    return x + _dot(cat, wo_ref[...])


def _merge_prompt_kernel(x_ref, u_ref, halo_ref, a_ref, pw_ref, ps_ref, pn_ref, an_ref, wo_ref,
                         o_ref, ext_ref):
    i = pl.program_id(1)
    tm = u_ref.shape[0]
    ext_ref[0:HALO, :] = jnp.where(i > 0, halo_ref[...], 0.0)
    ext_ref[HALO:HALO + tm, :] = u_ref[...]
    pos = i * tm + lax.broadcasted_iota(jnp.int32, (tm, 1), 0)
    ds = []
    for g, w in enumerate(POOL_WINDOWS):
        sl = slice(g * POOL_GROUP_DIM, (g + 1) * POOL_GROUP_DIM)
        tok = ext_ref[HALO:HALO + tm, sl]
        acc = tok
        for k in range(1, w):
            acc = acc + ext_ref[HALO - k:HALO - k + tm, sl]
        cnt = jnp.minimum(pos + 1, w).astype(F32)
        ds.append(acc / cnt - tok)
    d = jnp.concatenate(ds, axis=-1)
    a = jnp.concatenate([a_ref[hd] for hd in range(N_HEADS)], axis=0).T
    o_ref[...] = _merge_tail(x_ref[...], d, a, pw_ref, ps_ref, pn_ref, an_ref, wo_ref)


def _merge_weight_specs(fix2, fix3):
    return [
        pl.BlockSpec((len(POOL_WINDOWS), POOL_GROUP_DIM, POOL_GROUP_DIM), fix3),
        pl.BlockSpec((1, POOL_WIDTH), fix2),
        pl.BlockSpec((1, POOL_WIDTH), fix2),
        pl.BlockSpec((1, ATTN_WIDTH), fix2),
        pl.BlockSpec((D_MODEL, D_MODEL), fix2),
    ]


def _merge_prompt(x, u, attn, w, *, batch, t_pad, tm):
    n = x.shape[0]
    nt = t_pad // tm
    row = lambda b, i: (b * nt + i, 0)
    return pl.pallas_call(
        _merge_prompt_kernel,
        out_shape=jax.ShapeDtypeStruct((n, D_MODEL), F32),
        grid=(batch, nt),
        in_specs=[
            pl.BlockSpec((tm, D_MODEL), row),
            pl.BlockSpec((tm, POOL_WIDTH), row),
            pl.BlockSpec((HALO, POOL_WIDTH),
                         lambda b, i: (jnp.maximum((b * nt + i) * (tm // HALO) - 1, 0), 0)),
            pl.BlockSpec((N_HEADS, None, V_DIM, tm), lambda b, i: (0, b * nt + i, 0, 0)),
        ] + _merge_weight_specs(lambda b, i: (0, 0), lambda b, i: (0, 0, 0)),
        out_specs=pl.BlockSpec((tm, D_MODEL), row),
        scratch_shapes=[pltpu.VMEM((HALO + tm, POOL_WIDTH), F32)],
        compiler_params=_params(("parallel", "arbitrary")),
        name="merge_prompt",
    )(x, u, u, attn, w["pool_w"], w["pool_scale"], w["pool_out_norm"], w["attn_out_norm"], w["w_out"])


def _merge_sample_kernel(x_ref, st_ref, us_ref, pc_ref, wv_ref, pw_ref, ps_ref, pn_ref, an_ref, wo_ref,
                         o_ref):
    def ext_row(j, sl):
        return st_ref[j, :, sl] if j < POOL_STATE else us_ref[j - POOL_STATE, :, sl]

    rows = []
    for s in range(us_ref.shape[0]):
        ds = []
        for g, w in enumerate(POOL_WINDOWS):
            sl = slice(g * POOL_GROUP_DIM, (g + 1) * POOL_GROUP_DIM)
            tok = ext_row(POOL_STATE + s, sl)
            acc = tok
            for k in range(1, w):
                acc = acc + ext_row(POOL_STATE + s - k, sl)
            ds.append(acc * (1.0 / w) - tok)
        rows.append(jnp.concatenate(ds, axis=-1))
    d = jnp.concatenate(rows, axis=0)
    a = jnp.concatenate([_dot(pc_ref[hd].astype(BF16), wv_ref[hd]) for hd in range(N_HEADS)], axis=-1)
    o_ref[...] = _merge_tail(x_ref[...], d, a, pw_ref, ps_ref, pn_ref, an_ref, wo_ref)


def _merge_sample(x, state_t, us, pc, w, *, layer):
    n = x.shape[0]
    dec_seq, dec_batch, _ = us.shape
    once = pl.Buffered(1)
    return pl.pallas_call(
        _merge_sample_kernel,
        out_shape=jax.ShapeDtypeStruct((n, D_MODEL), F32),
        grid=(1,),
        in_specs=[
            pl.BlockSpec((n, D_MODEL), lambda s: (0, 0), pipeline_mode=once),
            pl.BlockSpec((None, POOL_STATE, dec_batch, POOL_WIDTH), lambda s: (layer, 0, 0, 0),
                         pipeline_mode=once),
            pl.BlockSpec((dec_seq, dec_batch, POOL_WIDTH), lambda s: (0, 0, 0), pipeline_mode=once),
            pl.BlockSpec((N_HEADS, n, KV_LORA), lambda s: (0, 0, 0), pipeline_mode=once),
            pl.BlockSpec((N_HEADS, KV_LORA, V_DIM), lambda s: (0, 0, 0), pipeline_mode=once),
            pl.BlockSpec((len(POOL_WINDOWS), POOL_GROUP_DIM, POOL_GROUP_DIM), lambda s: (0, 0, 0),
                         pipeline_mode=once),
            pl.BlockSpec((1, POOL_WIDTH), lambda s: (0, 0)),
            pl.BlockSpec((1, POOL_WIDTH), lambda s: (0, 0)),
            pl.BlockSpec((1, ATTN_WIDTH), lambda s: (0, 0)),
            pl.BlockSpec((D_MODEL, D_MODEL), lambda s: (0, 0), pipeline_mode=once),
        ],
        out_specs=pl.BlockSpec((n, D_MODEL), lambda s: (0, 0)),
        compiler_params=_params(("arbitrary",)),
        name="merge_sample",
    )(x, state_t, us, pc, w["wv3"], w["pool_w"], w["pool_scale"], w["pool_out_norm"],
      w["attn_out_norm"], w["w_out"])


def _qabs_kernel(q_ref, k_ref, wk_ref, gk_ref, qa_ref, qr_ref, sn_ref, *, dec_seq):
    qb = q_ref[...]
    q = qb.astype(F32)
    gk = gk_ref[...]
    qn = (q[:, :NOPE_DIM] * gk[:, :NOPE_DIM]).astype(BF16)
    qa_ref[...] = _dot_nt(qn, wk_ref[...]).astype(BF16)
    qr_ref[...] = (q[:, NOPE_DIM:] * gk[:, NOPE_DIM:]).astype(BF16)
    nb = qb.shape[0] // dec_seq
    eye = (lax.broadcasted_iota(jnp.int32, (nb, nb), 0) == lax.broadcasted_iota(jnp.int32, (nb, nb), 1))
    for s in range(dec_seq):
        for j in range(dec_seq):
            r = s * dec_seq + j
            if j <= s:
                mm = _dot_nt(qb[s * nb:(s + 1) * nb], k_ref[j * nb:(j + 1) * nb, :])
                sn_ref[r:r + 1, :] = jnp.sum(jnp.where(eye, mm, 0.0), axis=0, keepdims=True)
            else:
                sn_ref[r:r + 1, :] = jnp.full((1, nb), NEG_INF, F32)


def _qabs(q, k, w, *, dec_seq):
    n = q.shape[1]
    head = lambda h: (h, 0, 0)
    return pl.pallas_call(
        functools.partial(_qabs_kernel, dec_seq=dec_seq),
        out_shape=(jax.ShapeDtypeStruct((N_HEADS, n, KV_LORA), BF16),
                   jax.ShapeDtypeStruct((N_HEADS, n, QK_PAD - NOPE_DIM), BF16),
                   jax.ShapeDtypeStruct((N_HEADS, dec_seq * dec_seq, n // dec_seq), F32)),
        grid=(N_HEADS,),
        in_specs=[
            pl.BlockSpec((None, n, QK_PAD), head),
            pl.BlockSpec((None, n, QK_PAD), head),
            pl.BlockSpec((None, KV_LORA, NOPE_DIM), head),
            pl.BlockSpec((1, QK_PAD), lambda h: (0, 0)),
        ],
        out_specs=(pl.BlockSpec((None, n, KV_LORA), head),
                   pl.BlockSpec((None, n, QK_PAD - NOPE_DIM), head),
                   pl.BlockSpec((None, dec_seq * dec_seq, n // dec_seq), head)),
        compiler_params=_params(("parallel",)),
        name="qabs",
    )(q, k, w["wk3"], w["gk"])


def _sattn_kernel(pt_ref, qa_ref, qr_ref, snew_ref, cnew_ref, wkt_ref, ckv_hbm, kpe_hbm, o_ref,
                  a_ref, cbuf, pbuf, cb16, s_all, sem, *, layer, n_pages, n_rows):
    b = pl.program_id(0)
    nb = pl.num_programs(0)
    slot = b % 2
    n_kn = N_HEADS * NOPE_DIM

    def page_copies(page, p, sl):
        return (pltpu.make_async_copy(ckv_hbm.at[layer, page], cbuf.at[sl, p], sem.at[0, sl]),
                pltpu.make_async_copy(kpe_hbm.at[layer, page], pbuf.at[sl, p], sem.at[1, sl]))

    def issue(seq, sl):
        def body(p, carry):
            for cp in page_copies(pt_ref[seq, p], p, sl):
                cp.start()
            return carry
        lax.fori_loop(0, n_pages, body, 0, unroll=8)

    def wait(sl):
        for p in range(n_pages):
            for cp in page_copies(0, p, sl):
                cp.wait()

    @pl.when(b == 0)
    def _():
        a_ref[0:n_kn, :] = wkt_ref[...]
        issue(0, 0)

    @pl.when(b + 1 < nb)
    def _():
        issue(b + 1, 1 - slot)

    a_ref[n_kn:n_kn + n_rows, :] = qa_ref[...]
    qr = qr_ref[...]
    n_chunks = n_pages * PAGE_SIZE // KEY_CHUNK
    pages_per_chunk = KEY_CHUNK // PAGE_SIZE

    def scores(cb, pt):
        nk = cb.shape[0]
        r_all = _dot_nt(a_ref[...], cb)
        kn = r_all[:n_kn]
        ssq = jnp.sum((kn * kn).reshape(N_HEADS, NOPE_DIM, nk), axis=1)
        pe2 = jnp.sum(pt * pt, axis=0, keepdims=True)
        r = lax.rsqrt((ssq + pe2) * (1.0 / QK_DIM) + EPS)
        raw_r = _dot(qr, pt.astype(BF16))
        return (r_all[n_kn:] + raw_r) * jnp.concatenate([r] * (n_rows // N_HEADS), axis=0)

    def score_group(g):
        for j in range(CHUNK_GROUP):
            c = g * CHUNK_GROUP + j
            cb = cbuf[slot, pl.ds(c * pages_per_chunk, pages_per_chunk)] \
                .reshape(KEY_CHUNK, KV_LORA).astype(BF16)
            cb16[pl.ds(pl.multiple_of(c * KEY_CHUNK, KEY_CHUNK), KEY_CHUNK), :] = cb
            pt = jnp.concatenate([pbuf[slot, c * pages_per_chunk + k] for k in range(pages_per_chunk)],
                                 axis=1)
            s_all[c] = scores(cb, pt)

    def accumulate(s, c_rows, carry):
        m, l, acc = carry
        m_new = jnp.maximum(m, jnp.max(s, axis=-1, keepdims=True))
        p = jnp.exp2(s - m_new)
        alpha = jnp.exp2(m - m_new)
        l = alpha * l + jnp.sum(p, axis=-1, keepdims=True)
        acc = alpha * acc + _dot(p.astype(BF16), c_rows)
        return m_new, l, acc

    def accumulate_group(g, carry):
        s = jnp.concatenate([s_all[g * CHUNK_GROUP + j] for j in range(CHUNK_GROUP)], axis=1)
        n_g = CHUNK_GROUP * KEY_CHUNK
        return accumulate(s, cb16[pl.ds(pl.multiple_of(g * n_g, n_g), n_g), :], carry)

    wait(slot)
    n_groups = n_chunks // CHUNK_GROUP
    score_group(0)

    def body(g, carry):
        carry = accumulate_group(g - 1, carry)
        score_group(g)
        return carry

    carry = (jnp.full((n_rows, 1), -jnp.inf, F32), jnp.zeros((n_rows, 1), F32),
             jnp.zeros((n_rows, KV_LORA), F32))
    carry = lax.fori_loop(1, n_groups, body, carry)
    carry = accumulate_group(n_groups - 1, carry)
    _, l, acc = accumulate(snew_ref[...], cnew_ref[...], carry)
    o_ref[...] = acc / l


def _sattn(page_table, qa, qr, snew, cnew, wkt, cache_ckv, cache_kpe_t, *, layer):
    nb, n_pages = page_table.shape
    n_rows = qa.shape[1]
    n_keys = n_pages * PAGE_SIZE
    n_chunks = n_keys // KEY_CHUNK
    assert n_chunks % CHUNK_GROUP == 0 and KEY_CHUNK % PAGE_SIZE == 0
    seq3 = lambda b, pt: (b, 0, 0)
    grid_spec = pltpu.PrefetchScalarGridSpec(
        num_scalar_prefetch=1,
        grid=(nb,),
        in_specs=[
            pl.BlockSpec((None, n_rows, KV_LORA), seq3),
            pl.BlockSpec((None, n_rows, ROPE_DIM), seq3),
            pl.BlockSpec((None, n_rows, PAGE_SIZE), seq3),
            pl.BlockSpec((None, PAGE_SIZE, KV_LORA), seq3),
            pl.BlockSpec((N_HEADS * NOPE_DIM, KV_LORA), lambda b, pt: (0, 0)),
            pl.BlockSpec(memory_space=pl.ANY),
            pl.BlockSpec(memory_space=pl.ANY),
        ],
        out_specs=pl.BlockSpec((None, n_rows, KV_LORA), seq3),
        scratch_shapes=[
            pltpu.VMEM((N_HEADS * NOPE_DIM + n_rows, KV_LORA), BF16),
            pltpu.VMEM((2, n_pages, PAGE_SIZE, KV_LORA), F32),
            pltpu.VMEM((2, n_pages, ROPE_DIM, PAGE_SIZE), F32),
            pltpu.VMEM((n_keys, KV_LORA), BF16),
            pltpu.VMEM((n_chunks, n_rows, KEY_CHUNK), F32),
            pltpu.SemaphoreType.DMA((2, 2)),
        ],
    )
    return pl.pallas_call(
        functools.partial(_sattn_kernel, layer=layer, n_pages=n_pages, n_rows=n_rows),
        out_shape=jax.ShapeDtypeStruct((nb, n_rows, KV_LORA), F32),
        grid_spec=grid_spec,
        compiler_params=_params(("arbitrary",)),
        name="sattn",
    )(page_table, qa, qr, snew, cnew, wkt, cache_ckv, cache_kpe_t)


def _rot_half_cols(w):
    return jnp.concatenate([-w[..., ROPE_HALF:], w[..., :ROPE_HALF]], axis=-1)


def _rope_table(pos):
    inv = ROPE_THETA ** (-jnp.arange(ROPE_HALF, dtype=F32) / ROPE_HALF)
    ang = pos.astype(F32)[:, None] * inv[None, :]
    c, s = jnp.cos(ang), jnp.sin(ang)
    return jnp.concatenate([c, c, s, s], axis=-1)


def _layer_weights(l, p):
    row = lambda v: v[l][None, :].astype(F32)
    wq = p["w_q_b"][l].reshape(Q_LORA, N_HEADS, QK_DIM)
    wq_rope = wq[..., NOPE_DIM:]
    wkv = p["w_kv_b"][l].reshape(KV_LORA, N_HEADS, NOPE_DIM + V_DIM)
    wk = wkv[..., :NOPE_DIM]
    wv = wkv[..., NOPE_DIM:]
    zeros = jnp.zeros((QK_PAD - QK_DIM,), F32)

    def head_gain(g_nope, g_rope):
        return jnp.concatenate([g_nope[l], g_rope[l], g_rope[l], zeros])[None, :]

    out = {
        "mix_norm": row(p["mix_norm"]),
        "q_a_norm": row(p["q_a_norm"]),
        "wq": jnp.concatenate([wq, _rot_half_cols(wq_rope)], axis=-1)
              .reshape(Q_LORA, N_HEADS * QK_PAD).astype(BF16),
        "kv_a_norm": row(p["kv_a_norm"]),
        "wk": wk.reshape(KV_LORA, N_HEADS * NOPE_DIM).astype(BF16),
        "wvt": wv.reshape(KV_LORA, N_HEADS * V_DIM).T.astype(BF16),
        "wk3": wk.transpose(1, 0, 2).astype(BF16),
        "wkt": wk.reshape(KV_LORA, N_HEADS * NOPE_DIM).T.astype(BF16),
        "wv3": wv.transpose(1, 0, 2).astype(BF16),
        "gq": head_gain(p["q_norm_nope"], p["q_norm_rope"]) * (ATTN_SCALE * LOG2_E),
        "gk": head_gain(p["k_norm_nope"], p["k_norm_rope"]),
        "pool_w": p["pool_w"][l].astype(BF16),
        "pool_scale": row(p["pool_scale"]),
        "pool_out_norm": row(p["pool_out_norm"]),
        "attn_out_norm": row(p["attn_out_norm"]),
        "w_out": p["w_out"][l].astype(BF16),
        "ffn1_norm": row(p["ffn1_norm"]),
        "ffn2_norm": row(p["ffn2_norm"]),
    }
    return out


def kernel(x_prompt, x_sample, cache_ckv, cache_kpe, state_pool, page_table, meta_tokens, ffn1_norm, ffn1_w_gate, ffn1_w_up, ffn1_w_down, mix_norm, w_in, pool_w, pool_scale, q_a_norm, w_q_b, kv_a_norm, w_kv_b, q_norm_nope, q_norm_rope, k_norm_nope, k_norm_rope, pool_out_norm, attn_out_norm, w_out, ffn2_norm, ffn2_w_gate, ffn2_w_up, ffn2_w_down):
    p = dict(ffn1_norm=ffn1_norm, ffn1_w_gate=ffn1_w_gate, ffn1_w_up=ffn1_w_up, ffn1_w_down=ffn1_w_down,
             mix_norm=mix_norm, w_in=w_in, pool_w=pool_w, pool_scale=pool_scale, q_a_norm=q_a_norm,
             w_q_b=w_q_b, kv_a_norm=kv_a_norm, w_kv_b=w_kv_b, q_norm_nope=q_norm_nope,
             q_norm_rope=q_norm_rope, k_norm_nope=k_norm_nope, k_norm_rope=k_norm_rope,
             pool_out_norm=pool_out_norm, attn_out_norm=attn_out_norm, w_out=w_out,
             ffn2_norm=ffn2_norm, ffn2_w_gate=ffn2_w_gate, ffn2_w_up=ffn2_w_up, ffn2_w_down=ffn2_w_down)
    depth = w_in.shape[0]
    batch, seq, _ = x_prompt.shape
    dec_batch, dec_seq, _ = x_sample.shape
    n_pages = page_table.shape[1]
    t_real = N_META + seq
    t_pad = -(-t_real // ATT_BLOCK) * ATT_BLOCK
    n_p = batch * t_pad
    n_s = dec_seq * dec_batch
    assert n_p % FFN_ROWS == 0 and dec_batch % 8 == 0 and dec_seq * N_HEADS % 8 == 0
    cache_kpe_t = jnp.swapaxes(cache_kpe, 2, 3)
    state_t = jnp.swapaxes(state_pool.astype(F32), 1, 2)
    w_in_t = jnp.swapaxes(w_in, 1, 2).astype(BF16)
    w_pe_t = w_in_t[:, POOL_WIDTH + Q_LORA + KV_LORA:]
    w_in_t = jnp.concatenate([w_in_t, -w_pe_t[:, ROPE_HALF:], w_pe_t[:, :ROPE_HALF]], axis=1)

    meta = jnp.broadcast_to(meta_tokens[None].astype(F32), (batch, N_META, D_MODEL))
    xp = jnp.concatenate([meta, x_prompt, jnp.zeros((batch, t_pad - t_real, D_MODEL), F32)], axis=1)
    xp = xp.reshape(n_p, D_MODEL)
    xs = x_sample.transpose(1, 0, 2).reshape(n_s, D_MODEL)
    cs_p = jnp.tile(_rope_table(jnp.arange(t_pad)), (batch, 1))
    cs_s = jnp.repeat(_rope_table(n_pages * PAGE_SIZE + jnp.arange(dec_seq)), dec_batch, axis=0)

    outs = [[] for _ in range(6)]
    for l in range(depth):
        w = _layer_weights(l, p)
        xs, *w16 = _ffn_cast(xs, w["ffn1_norm"], ffn1_w_gate, ffn1_w_up, ffn1_w_down, layer=l)
        xp = _ffn(xp, w["ffn1_norm"], *w16, tm=FFN_ROWS)

        u_p, ckv_p, kpe_p, q_p, k_p, v_p = _inproj(xp, cs_p, w_in_t, w, layer=l)
        attn_p = _flash(q_p, k_p, v_p, batch=batch, t_pad=t_pad)
        xp = _merge_prompt(xp, u_p, attn_p, w, batch=batch, t_pad=t_pad, tm=ATT_BLOCK)
        outs[0].append(ckv_p.reshape(batch, t_pad, KV_LORA)[:, :t_real])
        outs[1].append(kpe_p.reshape(batch, t_pad, 2 * ROPE_DIM)[:, :t_real, :ROPE_DIM])
        outs[2].append(u_p.reshape(batch, t_pad, POOL_WIDTH)[:, t_real - POOL_STATE:t_real])

        u_s, ckv_s, kpe_s, q_s, k_s, _ = _inproj(xs, cs_s, w_in_t, w, layer=l)
        qa, qr, sn = _qabs(q_s, k_s, w, dec_seq=dec_seq)

        def per_seq(t):
            return t.reshape(N_HEADS, dec_seq, dec_batch, -1).transpose(2, 1, 0, 3) \
                    .reshape(dec_batch, dec_seq * N_HEADS, -1)

        c_new = ckv_s.reshape(dec_seq, dec_batch, KV_LORA).transpose(1, 0, 2)
        p_new = kpe_s[:, :ROPE_DIM].reshape(dec_seq, dec_batch, ROPE_DIM).transpose(1, 0, 2)
        n_fill = PAGE_SIZE - dec_seq
        sn = sn.reshape(N_HEADS, dec_seq, dec_seq, dec_batch).transpose(3, 1, 0, 2) \
               .reshape(dec_batch, dec_seq * N_HEADS, dec_seq)
        pc = _sattn(page_table, per_seq(qa), per_seq(qr[..., :ROPE_DIM]),
                    jnp.pad(sn, ((0, 0), (0, 0), (0, n_fill)), constant_values=NEG_INF),
                    jnp.pad(c_new, ((0, 0), (0, n_fill), (0, 0))).astype(BF16),
                    w["wkt"], cache_ckv, cache_kpe_t, layer=l)
        pc = pc.reshape(dec_batch, dec_seq, N_HEADS, KV_LORA).transpose(2, 1, 0, 3) \
               .reshape(N_HEADS, n_s, KV_LORA)
        us = u_s.reshape(dec_seq, dec_batch, POOL_WIDTH)
        xs = _merge_sample(xs, state_t, us, pc, w, layer=l)
        outs[3].append(c_new)
        outs[4].append(p_new)
        outs[5].append(jnp.concatenate([state_t[l, dec_seq:], us], axis=0).transpose(1, 0, 2))

        xs, *w16 = _ffn_cast(xs, w["ffn2_norm"], ffn2_w_gate, ffn2_w_up, ffn2_w_down, layer=l)
        xp = _ffn(xp, w["ffn2_norm"], *w16, tm=FFN_ROWS)

    y_prompt = xp.reshape(batch, t_pad, D_MODEL)[:, N_META:t_real]
    y_sample = xs.reshape(dec_seq, dec_batch, D_MODEL).transpose(1, 0, 2)
    return (y_prompt, y_sample) + tuple(jnp.stack(o) for o in outs)
```

```python
import functools

import jax
import jax.numpy as jnp
from jax import lax
from jax.experimental import pallas as pl
from jax.experimental.pallas import tpu as pltpu

F32 = jnp.float32
BF16 = jnp.bfloat16

D_MODEL = 2048
N_META = 16
POOL_WIDTH = 1024
POOL_WINDOWS = (2, 4, 8, 16)
POOL_GROUP_DIM = 256
POOL_STATE = 15
N_HEADS = 8
NOPE_DIM = 128
ROPE_DIM = 64
ROPE_HALF = 32
QK_DIM = 192
QK_PAD = 256
V_DIM = 128
ATTN_WIDTH = 1024
Q_LORA = 512
KV_LORA = 256
D_FF = 5632
ROPE_THETA = 10000.0
EPS = 1e-6
ATTN_SCALE = QK_DIM ** -0.5
LOG2_E = 1.4426950408889634
NEG_INF = -1e30
PAGE_SIZE = 128

IN_EXT = POOL_WIDTH + Q_LORA + KV_LORA + 2 * ROPE_DIM
FF_TILE = 512
FF_CAST_TILE = 256
FFN_ROWS = 544
ATT_BLOCK = 256
FLASH_HEADS = 4
HALO = 16
KEY_CHUNK = 256
CHUNK_GROUP = 8
VMEM_LIMIT = 56 * 1024 * 1024


def _rms(x, g):
    return x * lax.rsqrt(jnp.mean(x * x, axis=-1, keepdims=True) + EPS) * g


def _dot(a, b):
    return jnp.dot(a, b, preferred_element_type=F32)


def _dot_nt(a, b):
    return lax.dot_general(a, b, (((1,), (1,)), ((), ())), preferred_element_type=F32)


def _params(sem, vmem=VMEM_LIMIT, flags=None):
    return pltpu.CompilerParams(dimension_semantics=sem, vmem_limit_bytes=vmem, flags=flags)


def _ffn_init(x_ref, g_ref, o_ref, h_ref):
    @pl.when(pl.program_id(1) == 0)
    def _():
        x = x_ref[...]
        h_ref[...] = _rms(x, g_ref[...]).astype(BF16)
        o_ref[...] = x


def _ffn_kernel(x_ref, g_ref, wg_ref, wu_ref, wd_ref, o_ref, h_ref):
    _ffn_init(x_ref, g_ref, o_ref, h_ref)
    h = h_ref[...]
    gate = _dot(h, wg_ref[...])
    up = _dot(h, wu_ref[...])
    a = (0.5 * gate * jax.nn.sigmoid(gate) * up).astype(BF16)
    o_ref[...] += _dot(a, wd_ref[...])


def _ffn_cast_kernel(x_ref, g_ref, wg_ref, wu_ref, wd_ref, o_ref, wg16_ref, wu16_ref, wd16_ref, h_ref):
    _ffn_init(x_ref, g_ref, o_ref, h_ref)
    h = h_ref[...]
    wg = wg_ref[...].astype(BF16)
    wg16_ref[...] = wg
    gate = _dot(h, wg)
    wu = wu_ref[...].astype(BF16)
    wu16_ref[...] = wu
    up = _dot(h, wu)
    a = (0.5 * gate * jax.nn.sigmoid(gate) * up).astype(BF16)
    wd = wd_ref[...].astype(BF16)
    wd16_ref[...] = wd
    o_ref[...] += _dot(a, wd)


def _ffn(x, g, wg, wu, wd, *, tm):
    n = x.shape[0]
    return pl.pallas_call(
        _ffn_kernel,
        out_shape=jax.ShapeDtypeStruct((n, D_MODEL), F32),
        grid=(n // tm, D_FF // FF_TILE),
        in_specs=[
            pl.BlockSpec((tm, D_MODEL), lambda i, j: (i, 0)),
            pl.BlockSpec((1, D_MODEL), lambda i, j: (0, 0)),
            pl.BlockSpec((D_MODEL, FF_TILE), lambda i, j: (0, j)),
            pl.BlockSpec((D_MODEL, FF_TILE), lambda i, j: (0, j)),
            pl.BlockSpec((FF_TILE, D_MODEL), lambda i, j: (j, 0)),
        ],
        out_specs=pl.BlockSpec((tm, D_MODEL), lambda i, j: (i, 0)),
        scratch_shapes=[pltpu.VMEM((tm, D_MODEL), BF16)],
        compiler_params=_params(("parallel", "arbitrary")),
        name="ffn",
    )(x, g, wg, wu, wd)


def _ffn_cast(x, g, wg, wu, wd, *, layer):
    n = x.shape[0]
    tf = FF_CAST_TILE
    w_in_spec = pl.BlockSpec((None, D_MODEL, tf), lambda i, j: (layer, 0, j))
    w_out_spec = pl.BlockSpec((D_MODEL, tf), lambda i, j: (0, j))
    return pl.pallas_call(
        _ffn_cast_kernel,
        out_shape=(jax.ShapeDtypeStruct((n, D_MODEL), F32),
                   jax.ShapeDtypeStruct((D_MODEL, D_FF), BF16),
                   jax.ShapeDtypeStruct((D_MODEL, D_FF), BF16),
                   jax.ShapeDtypeStruct((D_FF, D_MODEL), BF16)),
        grid=(1, D_FF // tf),
        in_specs=[
            pl.BlockSpec((n, D_MODEL), lambda i, j: (0, 0)),
            pl.BlockSpec((1, D_MODEL), lambda i, j: (0, 0)),
            w_in_spec,
            w_in_spec,
            pl.BlockSpec((None, tf, D_MODEL), lambda i, j: (layer, j, 0)),
        ],
        out_specs=(pl.BlockSpec((n, D_MODEL), lambda i, j: (0, 0)),
                   w_out_spec,
                   w_out_spec,
                   pl.BlockSpec((tf, D_MODEL), lambda i, j: (j, 0))),
        scratch_shapes=[pltpu.VMEM((n, D_MODEL), BF16)],
        compiler_params=_params(("arbitrary", "arbitrary")),
        name="ffn_cast",
    )(x, g, wg, wu, wd)


def _inproj_kernel(x_ref, gmix_ref, win_ref, gqa_ref, wq_ref, gkva_ref, wk_ref, wv_ref,
                   cs_ref, gq_ref, gk_ref,
                   u_ref, ckv_ref, kpe_ref, q_ref, k_ref, v_ref):
    h = _rms(x_ref[...], gmix_ref[...]).astype(BF16)
    z = _dot_nt(h, win_ref[...])
    u_ref[...] = z[:, :POOL_WIDTH]
    ql = _rms(z[:, POOL_WIDTH:POOL_WIDTH + Q_LORA], gqa_ref[...]).astype(BF16)
    o_kv = POOL_WIDTH + Q_LORA
    c = _rms(z[:, o_kv:o_kv + KV_LORA], gkva_ref[...])
    ckv_ref[...] = c

    cs = cs_ref[...]
    lane = lax.broadcasted_iota(jnp.int32, cs.shape, 1)
    low = lane < ROPE_DIM

    def rope(t):
        t = t * cs
        return jnp.where(low, t + pltpu.roll(t, ROPE_DIM, axis=1), 0.0)

    kpe = rope(z[:, o_kv + KV_LORA:])
    kpe_ref[...] = kpe
    kpe_ss = jnp.sum(kpe * kpe, axis=-1, keepdims=True)

    q = _dot(ql, wq_ref[...])
    cb = c.astype(BF16)
    kn = _dot(cb, wk_ref[...])
    vt = _dot_nt(wv_ref[...], cb)
    gq = gq_ref[...]
    gk = gk_ref[...]
    inv_d = 1.0 / QK_DIM
    for hd in range(N_HEADS):
        qa = q[:, hd * QK_PAD:hd * QK_PAD + NOPE_DIM]
        qb = rope(q[:, hd * QK_PAD + NOPE_DIM:(hd + 1) * QK_PAD])
        rs = lax.rsqrt(jnp.sum(qa * qa + qb * qb, axis=-1, keepdims=True) * inv_d + EPS)
        q_ref[hd, :, :NOPE_DIM] = (qa * rs * gq[:, :NOPE_DIM]).astype(BF16)
        q_ref[hd, :, NOPE_DIM:] = (qb * rs * gq[:, NOPE_DIM:]).astype(BF16)
        ka = kn[:, hd * NOPE_DIM:(hd + 1) * NOPE_DIM]
        rk = lax.rsqrt((jnp.sum(ka * ka, axis=-1, keepdims=True) + kpe_ss) * inv_d + EPS)
        k_ref[hd, :, :NOPE_DIM] = (ka * rk * gk[:, :NOPE_DIM]).astype(BF16)
        k_ref[hd, :, NOPE_DIM:] = (kpe * rk * gk[:, NOPE_DIM:]).astype(BF16)
        v_ref[hd] = vt[hd * V_DIM:(hd + 1) * V_DIM, :].astype(BF16)


def _inproj(x, cs, w_in_t, w, *, layer):
    n = x.shape[0]
    tm = ATT_BLOCK
    row = lambda i: (i, 0)
    fix = lambda i: (0, 0)
    head = lambda i: (0, i, 0)
    return pl.pallas_call(
        _inproj_kernel,
        out_shape=(
            jax.ShapeDtypeStruct((n, POOL_WIDTH), F32),
            jax.ShapeDtypeStruct((n, KV_LORA), F32),
            jax.ShapeDtypeStruct((n, 2 * ROPE_DIM), F32),
            jax.ShapeDtypeStruct((N_HEADS, n, QK_PAD), BF16),
            jax.ShapeDtypeStruct((N_HEADS, n, QK_PAD), BF16),
            jax.ShapeDtypeStruct((N_HEADS, n // tm, V_DIM, tm), BF16),
        ),
        grid=(n // tm,),
        in_specs=[
            pl.BlockSpec((tm, D_MODEL), row),
            pl.BlockSpec((1, D_MODEL), fix),
            pl.BlockSpec((None, IN_EXT, D_MODEL), lambda i: (layer, 0, 0)),
            pl.BlockSpec((1, Q_LORA), fix),
            pl.BlockSpec((Q_LORA, N_HEADS * QK_PAD), fix),
            pl.BlockSpec((1, KV_LORA), fix),
            pl.BlockSpec((KV_LORA, N_HEADS * NOPE_DIM), fix),
            pl.BlockSpec((N_HEADS * V_DIM, KV_LORA), fix),
            pl.BlockSpec((tm, 2 * ROPE_DIM), row),
            pl.BlockSpec((1, QK_PAD), fix),
            pl.BlockSpec((1, QK_PAD), fix),
        ],
        out_specs=(
            pl.BlockSpec((tm, POOL_WIDTH), row),
            pl.BlockSpec((tm, KV_LORA), row),
            pl.BlockSpec((tm, 2 * ROPE_DIM), row),
            pl.BlockSpec((N_HEADS, tm, QK_PAD), head),
            pl.BlockSpec((N_HEADS, tm, QK_PAD), head),
            pl.BlockSpec((N_HEADS, None, V_DIM, tm), lambda i: (0, i, 0, 0)),
        ),
        compiler_params=_params(("parallel",)),
        name="inproj",
    )(x, w["mix_norm"], w_in_t, w["q_a_norm"], w["wq"], w["kv_a_norm"], w["wk"], w["wvt"],
      cs, w["gq"], w["gk"])


def _flash_kernel(q_ref, k_ref, v_ref, o_ref):
    i = pl.program_id(2)
    nh, bq, _ = q_ref.shape

    def scores(hd, j):
        off = pl.multiple_of(j * ATT_BLOCK, ATT_BLOCK)
        return _dot_nt(k_ref[hd, pl.ds(off, ATT_BLOCK), :], q_ref[hd])

    def update(hd, j, st, stats):
        m, l, acc = stats
        m_new = jnp.maximum(m, jnp.max(st, axis=0, keepdims=True))
        p = jnp.exp2(st - m_new)
        alpha = jnp.exp2(m - m_new)
        l = alpha * l + jnp.sum(p, axis=0, keepdims=True)
        acc = alpha * acc + _dot(v_ref[hd, j], p.astype(BF16))
        return m_new, l, acc

    def body(j, carry):
        return tuple((scores(hd, j + 1), update(hd, j, *carry[hd])) for hd in range(nh))

    init = (jnp.full((1, bq), -jnp.inf, F32), jnp.zeros((1, bq), F32), jnp.zeros((V_DIM, bq), F32))
    carry = lax.fori_loop(0, i, body, tuple((scores(hd, 0), init) for hd in range(nh)))
    key = lax.broadcasted_iota(jnp.int32, (ATT_BLOCK, bq), 0)
    qry = lax.broadcasted_iota(jnp.int32, (ATT_BLOCK, bq), 1)
    for hd in range(nh):
        st, stats = carry[hd]
        _, l, acc = update(hd, i, jnp.where(key <= qry, st, NEG_INF), stats)
        o_ref[hd] = acc / l


def _flash(q, k, v, *, batch, t_pad):
    nq = t_pad // ATT_BLOCK
    qmap = lambda b, h, i: (h, b * nq + i, 0)
    return pl.pallas_call(
        _flash_kernel,
        out_shape=jax.ShapeDtypeStruct(v.shape, F32),
        grid=(batch, N_HEADS // FLASH_HEADS, nq),
        in_specs=[
            pl.BlockSpec((FLASH_HEADS, ATT_BLOCK, QK_PAD), qmap),
            pl.BlockSpec((FLASH_HEADS, t_pad, QK_PAD), lambda b, h, i: (h, b, 0)),
            pl.BlockSpec((FLASH_HEADS, nq, V_DIM, ATT_BLOCK), lambda b, h, i: (h, b, 0, 0)),
        ],
        out_specs=pl.BlockSpec((FLASH_HEADS, None, V_DIM, ATT_BLOCK),
                               lambda b, h, i: (h, b * nq + i, 0, 0)),
        compiler_params=_params(("parallel", "parallel", "arbitrary")),
        name="flash",
    )(q, k, v)


def _merge_tail(x, d, a, pw_ref, ps_ref, pn_ref, an_ref, wo_ref):
    pool = jnp.concatenate(
        [_dot(d[:, g * POOL_GROUP_DIM:(g + 1) * POOL_GROUP_DIM].astype(BF16), pw_ref[g])
         for g in range(len(POOL_WINDOWS))], axis=-1) * ps_ref[...]
    cat = jnp.concatenate([_rms(pool, pn_ref[...]).astype(BF16),
                           _rms(a, an_ref[...]).astype(BF16)], axis=-1)
    return x + _dot(cat, wo_ref[...])


def _merge_prompt_kernel(x_ref, u_ref, halo_ref, a_ref, pw_ref, ps_ref, pn_ref, an_ref, wo_ref,
                         o_ref, ext_ref):
    i = pl.program_id(1)
    tm = u_ref.shape[0]
    ext_ref[0:HALO, :] = jnp.where(i > 0, halo_ref[...], 0.0)
    ext_ref[HALO:HALO + tm, :] = u_ref[...]
    pos = i * tm + lax.broadcasted_iota(jnp.int32, (tm, 1), 0)
    ds = []
    for g, w in enumerate(POOL_WINDOWS):
        sl = slice(g * POOL_GROUP_DIM, (g + 1) * POOL_GROUP_DIM)
        tok = ext_ref[HALO:HALO + tm, sl]
        acc = tok
        for k in range(1, w):
            acc = acc + ext_ref[HALO - k:HALO - k + tm, sl]
        cnt = jnp.minimum(pos + 1, w).astype(F32)
        ds.append(acc / cnt - tok)
    d = jnp.concatenate(ds, axis=-1)
    a = jnp.concatenate([a_ref[hd] for hd in range(N_HEADS)], axis=0).T
    o_ref[...] = _merge_tail(x_ref[...], d, a, pw_ref, ps_ref, pn_ref, an_ref, wo_ref)


def _merge_weight_specs(fix2, fix3):
    return [
        pl.BlockSpec((len(POOL_WINDOWS), POOL_GROUP_DIM, POOL_GROUP_DIM), fix3),
        pl.BlockSpec((1, POOL_WIDTH), fix2),
        pl.BlockSpec((1, POOL_WIDTH), fix2),
        pl.BlockSpec((1, ATTN_WIDTH), fix2),
        pl.BlockSpec((D_MODEL, D_MODEL), fix2),
    ]


def _merge_prompt(x, u, attn, w, *, batch, t_pad, tm):
    n = x.shape[0]
    nt = t_pad // tm
    row = lambda b, i: (b * nt + i, 0)
    return pl.pallas_call(
        _merge_prompt_kernel,
        out_shape=jax.ShapeDtypeStruct((n, D_MODEL), F32),
        grid=(batch, nt),
        in_specs=[
            pl.BlockSpec((tm, D_MODEL), row),
            pl.BlockSpec((tm, POOL_WIDTH), row),
            pl.BlockSpec((HALO, POOL_WIDTH),
                         lambda b, i: (jnp.maximum((b * nt + i) * (tm // HALO) - 1, 0), 0)),
            pl.BlockSpec((N_HEADS, None, V_DIM, tm), lambda b, i: (0, b * nt + i, 0, 0)),
        ] + _merge_weight_specs(lambda b, i: (0, 0), lambda b, i: (0, 0, 0)),
        out_specs=pl.BlockSpec((tm, D_MODEL), row),
        scratch_shapes=[pltpu.VMEM((HALO + tm, POOL_WIDTH), F32)],
        compiler_params=_params(("parallel", "arbitrary")),
        name="merge_prompt",
    )(x, u, u, attn, w["pool_w"], w["pool_scale"], w["pool_out_norm"], w["attn_out_norm"], w["w_out"])


def _merge_sample_kernel(x_ref, st_ref, us_ref, pc_ref, wv_ref, pw_ref, ps_ref, pn_ref, an_ref, wo_ref,
                         o_ref):
    def ext_row(j, sl):
        return st_ref[j, :, sl] if j < POOL_STATE else us_ref[j - POOL_STATE, :, sl]

    rows = []
    for s in range(us_ref.shape[0]):
        ds = []
        for g, w in enumerate(POOL_WINDOWS):
            sl = slice(g * POOL_GROUP_DIM, (g + 1) * POOL_GROUP_DIM)
            tok = ext_row(POOL_STATE + s, sl)
            acc = tok
            for k in range(1, w):
                acc = acc + ext_row(POOL_STATE + s - k, sl)
            ds.append(acc * (1.0 / w) - tok)
        rows.append(jnp.concatenate(ds, axis=-1))
    d = jnp.concatenate(rows, axis=0)
    a = jnp.concatenate([_dot(pc_ref[hd].astype(BF16), wv_ref[hd]) for hd in range(N_HEADS)], axis=-1)
    o_ref[...] = _merge_tail(x_ref[...], d, a, pw_ref, ps_ref, pn_ref, an_ref, wo_ref)


def _merge_sample(x, state_t, us, pc, w, *, layer):
    n = x.shape[0]
    dec_seq, dec_batch, _ = us.shape
    once = pl.Buffered(1)
    return pl.pallas_call(
        _merge_sample_kernel,
        out_shape=jax.ShapeDtypeStruct((n, D_MODEL), F32),
        grid=(1,),
        in_specs=[
            pl.BlockSpec((n, D_MODEL), lambda s: (0, 0), pipeline_mode=once),
            pl.BlockSpec((None, POOL_STATE, dec_batch, POOL_WIDTH), lambda s: (layer, 0, 0, 0),
                         pipeline_mode=once),
            pl.BlockSpec((dec_seq, dec_batch, POOL_WIDTH), lambda s: (0, 0, 0), pipeline_mode=once),
            pl.BlockSpec((N_HEADS, n, KV_LORA), lambda s: (0, 0, 0), pipeline_mode=once),
            pl.BlockSpec((N_HEADS, KV_LORA, V_DIM), lambda s: (0, 0, 0), pipeline_mode=once),
            pl.BlockSpec((len(POOL_WINDOWS), POOL_GROUP_DIM, POOL_GROUP_DIM), lambda s: (0, 0, 0),
                         pipeline_mode=once),
            pl.BlockSpec((1, POOL_WIDTH), lambda s: (0, 0)),
            pl.BlockSpec((1, POOL_WIDTH), lambda s: (0, 0)),
            pl.BlockSpec((1, ATTN_WIDTH), lambda s: (0, 0)),
            pl.BlockSpec((D_MODEL, D_MODEL), lambda s: (0, 0), pipeline_mode=once),
        ],
        out_specs=pl.BlockSpec((n, D_MODEL), lambda s: (0, 0)),
        compiler_params=_params(("arbitrary",)),
        name="merge_sample",
    )(x, state_t, us, pc, w["wv3"], w["pool_w"], w["pool_scale"], w["pool_out_norm"],
      w["attn_out_norm"], w["w_out"])


def _qabs_kernel(q_ref, k_ref, wk_ref, gk_ref, qa_ref, qr_ref, sn_ref, *, dec_seq):
    qb = q_ref[...]
    q = qb.astype(F32)
    gk = gk_ref[...]
    qn = (q[:, :NOPE_DIM] * gk[:, :NOPE_DIM]).astype(BF16)
    qa_ref[...] = _dot_nt(qn, wk_ref[...]).astype(BF16)
    qr_ref[...] = (q[:, NOPE_DIM:] * gk[:, NOPE_DIM:]).astype(BF16)
    nb = qb.shape[0] // dec_seq
    eye = (lax.broadcasted_iota(jnp.int32, (nb, nb), 0) == lax.broadcasted_iota(jnp.int32, (nb, nb), 1))
    for s in range(dec_seq):
        for j in range(dec_seq):
            r = s * dec_seq + j
            if j <= s:
                mm = _dot_nt(qb[s * nb:(s + 1) * nb], k_ref[j * nb:(j + 1) * nb, :])
                sn_ref[r:r + 1, :] = jnp.sum(jnp.where(eye, mm, 0.0), axis=0, keepdims=True)
            else:
                sn_ref[r:r + 1, :] = jnp.full((1, nb), NEG_INF, F32)


def _qabs(q, k, w, *, dec_seq):
    n = q.shape[1]
    head = lambda h: (h, 0, 0)
    return pl.pallas_call(
        functools.partial(_qabs_kernel, dec_seq=dec_seq),
        out_shape=(jax.ShapeDtypeStruct((N_HEADS, n, KV_LORA), BF16),
                   jax.ShapeDtypeStruct((N_HEADS, n, QK_PAD - NOPE_DIM), BF16),
                   jax.ShapeDtypeStruct((N_HEADS, dec_seq * dec_seq, n // dec_seq), F32)),
        grid=(N_HEADS,),
        in_specs=[
            pl.BlockSpec((None, n, QK_PAD), head),
            pl.BlockSpec((None, n, QK_PAD), head),
            pl.BlockSpec((None, KV_LORA, NOPE_DIM), head),
            pl.BlockSpec((1, QK_PAD), lambda h: (0, 0)),
        ],
        out_specs=(pl.BlockSpec((None, n, KV_LORA), head),
                   pl.BlockSpec((None, n, QK_PAD - NOPE_DIM), head),
                   pl.BlockSpec((None, dec_seq * dec_seq, n // dec_seq), head)),
        compiler_params=_params(("parallel",)),
        name="qabs",
    )(q, k, w["wk3"], w["gk"])


def _sattn_kernel(pt_ref, qa_ref, qr_ref, snew_ref, cnew_ref, wkt_ref, ckv_hbm, kpe_hbm, o_ref,
                  a_ref, cbuf, pbuf, cb16, s_all, sem, *, layer, n_pages, n_rows):
    b = pl.program_id(0)
    nb = pl.num_programs(0)
    slot = b % 2
    n_kn = N_HEADS * NOPE_DIM

    def page_copies(page, p, sl):
        return (pltpu.make_async_copy(ckv_hbm.at[layer, page], cbuf.at[sl, p], sem.at[0, sl]),
                pltpu.make_async_copy(kpe_hbm.at[layer, page], pbuf.at[sl, p], sem.at[1, sl]))

    def issue(seq, sl):
        def body(p, carry):
            for cp in page_copies(pt_ref[seq, p], p, sl):
                cp.start()
            return carry
        lax.fori_loop(0, n_pages, body, 0, unroll=8)

    def wait(sl):
        for p in range(n_pages):
            for cp in page_copies(0, p, sl):
                cp.wait()

    @pl.when(b == 0)
    def _():
        a_ref[0:n_kn, :] = wkt_ref[...]
        issue(0, 0)

    @pl.when(b + 1 < nb)
    def _():
        issue(b + 1, 1 - slot)

    a_ref[n_kn:n_kn + n_rows, :] = qa_ref[...]
    qr = qr_ref[...]
    n_chunks = n_pages * PAGE_SIZE // KEY_CHUNK
    pages_per_chunk = KEY_CHUNK // PAGE_SIZE

    def scores(cb, pt):
        nk = cb.shape[0]
        r_all = _dot_nt(a_ref[...], cb)
        kn = r_all[:n_kn]
        ssq = jnp.sum((kn * kn).reshape(N_HEADS, NOPE_DIM, nk), axis=1)
        pe2 = jnp.sum(pt * pt, axis=0, keepdims=True)
        r = lax.rsqrt((ssq + pe2) * (1.0 / QK_DIM) + EPS)
        raw_r = _dot(qr, pt.astype(BF16))
        return (r_all[n_kn:] + raw_r) * jnp.concatenate([r] * (n_rows // N_HEADS), axis=0)

    def score_group(g):
        for j in range(CHUNK_GROUP):
            c = g * CHUNK_GROUP + j
            cb = cbuf[slot, pl.ds(c * pages_per_chunk, pages_per_chunk)] \
                .reshape(KEY_CHUNK, KV_LORA).astype(BF16)
            cb16[pl.ds(pl.multiple_of(c * KEY_CHUNK, KEY_CHUNK), KEY_CHUNK), :] = cb
            pt = jnp.concatenate([pbuf[slot, c * pages_per_chunk + k] for k in range(pages_per_chunk)],
                                 axis=1)
            s_all[c] = scores(cb, pt)

    def accumulate(s, c_rows, carry):
        m, l, acc = carry
        m_new = jnp.maximum(m, jnp.max(s, axis=-1, keepdims=True))
        p = jnp.exp2(s - m_new)
        alpha = jnp.exp2(m - m_new)
        l = alpha * l + jnp.sum(p, axis=-1, keepdims=True)
        acc = alpha * acc + _dot(p.astype(BF16), c_rows)
        return m_new, l, acc

    def accumulate_group(g, carry):
        s = jnp.concatenate([s_all[g * CHUNK_GROUP + j] for j in range(CHUNK_GROUP)], axis=1)
        n_g = CHUNK_GROUP * KEY_CHUNK
        return accumulate(s, cb16[pl.ds(pl.multiple_of(g * n_g, n_g), n_g), :], carry)

    wait(slot)
    n_groups = n_chunks // CHUNK_GROUP
    score_group(0)

    def body(g, carry):
        carry = accumulate_group(g - 1, carry)
        score_group(g)
        return carry

    carry = (jnp.full((n_rows, 1), -jnp.inf, F32), jnp.zeros((n_rows, 1), F32),
             jnp.zeros((n_rows, KV_LORA), F32))
    carry = lax.fori_loop(1, n_groups, body, carry)
    carry = accumulate_group(n_groups - 1, carry)
    _, l, acc = accumulate(snew_ref[...], cnew_ref[...], carry)
    o_ref[...] = acc / l


def _sattn(page_table, qa, qr, snew, cnew, wkt, cache_ckv, cache_kpe_t, *, layer):
    nb, n_pages = page_table.shape
    n_rows = qa.shape[1]
    n_keys = n_pages * PAGE_SIZE
    n_chunks = n_keys // KEY_CHUNK
    assert n_chunks % CHUNK_GROUP == 0 and KEY_CHUNK % PAGE_SIZE == 0
    seq3 = lambda b, pt: (b, 0, 0)
    grid_spec = pltpu.PrefetchScalarGridSpec(
        num_scalar_prefetch=1,
        grid=(nb,),
        in_specs=[
            pl.BlockSpec((None, n_rows, KV_LORA), seq3),
            pl.BlockSpec((None, n_rows, ROPE_DIM), seq3),
            pl.BlockSpec((None, n_rows, PAGE_SIZE), seq3),
            pl.BlockSpec((None, PAGE_SIZE, KV_LORA), seq3),
            pl.BlockSpec((N_HEADS * NOPE_DIM, KV_LORA), lambda b, pt: (0, 0)),
            pl.BlockSpec(memory_space=pl.ANY),
            pl.BlockSpec(memory_space=pl.ANY),
        ],
        out_specs=pl.BlockSpec((None, n_rows, KV_LORA), seq3),
        scratch_shapes=[
            pltpu.VMEM((N_HEADS * NOPE_DIM + n_rows, KV_LORA), BF16),
            pltpu.VMEM((2, n_pages, PAGE_SIZE, KV_LORA), F32),
            pltpu.VMEM((2, n_pages, ROPE_DIM, PAGE_SIZE), F32),
            pltpu.VMEM((n_keys, KV_LORA), BF16),
            pltpu.VMEM((n_chunks, n_rows, KEY_CHUNK), F32),
            pltpu.SemaphoreType.DMA((2, 2)),
        ],
    )
    return pl.pallas_call(
        functools.partial(_sattn_kernel, layer=layer, n_pages=n_pages, n_rows=n_rows),
        out_shape=jax.ShapeDtypeStruct((nb, n_rows, KV_LORA), F32),
        grid_spec=grid_spec,
        compiler_params=_params(("arbitrary",)),
        name="sattn",
    )(page_table, qa, qr, snew, cnew, wkt, cache_ckv, cache_kpe_t)


def _rot_half_cols(w):
    return jnp.concatenate([-w[..., ROPE_HALF:], w[..., :ROPE_HALF]], axis=-1)


def _rope_table(pos):
    inv = ROPE_THETA ** (-jnp.arange(ROPE_HALF, dtype=F32) / ROPE_HALF)
    ang = pos.astype(F32)[:, None] * inv[None, :]
    c, s = jnp.cos(ang), jnp.sin(ang)
    return jnp.concatenate([c, c, s, s], axis=-1)


def _layer_weights(l, p):
    row = lambda v: v[l][None, :].astype(F32)
    wq = p["w_q_b"][l].reshape(Q_LORA, N_HEADS, QK_DIM)
    wq_rope = wq[..., NOPE_DIM:]
    wkv = p["w_kv_b"][l].reshape(KV_LORA, N_HEADS, NOPE_DIM + V_DIM)
    wk = wkv[..., :NOPE_DIM]
    wv = wkv[..., NOPE_DIM:]
    zeros = jnp.zeros((QK_PAD - QK_DIM,), F32)

    def head_gain(g_nope, g_rope):
        return jnp.concatenate([g_nope[l], g_rope[l], g_rope[l], zeros])[None, :]

    out = {
        "mix_norm": row(p["mix_norm"]),
        "q_a_norm": row(p["q_a_norm"]),
        "wq": jnp.concatenate([wq, _rot_half_cols(wq_rope)], axis=-1)
              .reshape(Q_LORA, N_HEADS * QK_PAD).astype(BF16),
        "kv_a_norm": row(p["kv_a_norm"]),
        "wk": wk.reshape(KV_LORA, N_HEADS * NOPE_DIM).astype(BF16),
        "wvt": wv.reshape(KV_LORA, N_HEADS * V_DIM).T.astype(BF16),
        "wk3": wk.transpose(1, 0, 2).astype(BF16),
        "wkt": wk.reshape(KV_LORA, N_HEADS * NOPE_DIM).T.astype(BF16),
        "wv3": wv.transpose(1, 0, 2).astype(BF16),
        "gq": head_gain(p["q_norm_nope"], p["q_norm_rope"]) * (ATTN_SCALE * LOG2_E),
        "gk": head_gain(p["k_norm_nope"], p["k_norm_rope"]),
        "pool_w": p["pool_w"][l].astype(BF16),
        "pool_scale": row(p["pool_scale"]),
        "pool_out_norm": row(p["pool_out_norm"]),
        "attn_out_norm": row(p["attn_out_norm"]),
        "w_out": p["w_out"][l].astype(BF16),
        "ffn1_norm": row(p["ffn1_norm"]),
        "ffn2_norm": row(p["ffn2_norm"]),
    }
    return out


def kernel(x_prompt, x_sample, cache_ckv, cache_kpe, state_pool, page_table, meta_tokens, ffn1_norm, ffn1_w_gate, ffn1_w_up, ffn1_w_down, mix_norm, w_in, pool_w, pool_scale, q_a_norm, w_q_b, kv_a_norm, w_kv_b, q_norm_nope, q_norm_rope, k_norm_nope, k_norm_rope, pool_out_norm, attn_out_norm, w_out, ffn2_norm, ffn2_w_gate, ffn2_w_up, ffn2_w_down):
    p = dict(ffn1_norm=ffn1_norm, ffn1_w_gate=ffn1_w_gate, ffn1_w_up=ffn1_w_up, ffn1_w_down=ffn1_w_down,
             mix_norm=mix_norm, w_in=w_in, pool_w=pool_w, pool_scale=pool_scale, q_a_norm=q_a_norm,
             w_q_b=w_q_b, kv_a_norm=kv_a_norm, w_kv_b=w_kv_b, q_norm_nope=q_norm_nope,
             q_norm_rope=q_norm_rope, k_norm_nope=k_norm_nope, k_norm_rope=k_norm_rope,
             pool_out_norm=pool_out_norm, attn_out_norm=attn_out_norm, w_out=w_out,
             ffn2_norm=ffn2_norm, ffn2_w_gate=ffn2_w_gate, ffn2_w_up=ffn2_w_up, ffn2_w_down=ffn2_w_down)
    depth = w_in.shape[0]
    batch, seq, _ = x_prompt.shape
    dec_batch, dec_seq, _ = x_sample.shape
    n_pages = page_table.shape[1]
    t_real = N_META + seq
    t_pad = -(-t_real // ATT_BLOCK) * ATT_BLOCK
    n_p = batch * t_pad
    n_s = dec_seq * dec_batch
    assert n_p % FFN_ROWS == 0 and dec_batch % 8 == 0 and dec_seq * N_HEADS % 8 == 0
    cache_kpe_t = jnp.swapaxes(cache_kpe, 2, 3)
    state_t = jnp.swapaxes(state_pool.astype(F32), 1, 2)
    w_in_t = jnp.swapaxes(w_in, 1, 2).astype(BF16)
    w_pe_t = w_in_t[:, POOL_WIDTH + Q_LORA + KV_LORA:]
    w_in_t = jnp.concatenate([w_in_t, -w_pe_t[:, ROPE_HALF:], w_pe_t[:, :ROPE_HALF]], axis=1)

    meta = jnp.broadcast_to(meta_tokens[None].astype(F32), (batch, N_META, D_MODEL))
    xp = jnp.concatenate([meta, x_prompt, jnp.zeros((batch, t_pad - t_real, D_MODEL), F32)], axis=1)
    xp = xp.reshape(n_p, D_MODEL)
    xs = x_sample.transpose(1, 0, 2).reshape(n_s, D_MODEL)
    cs_p = jnp.tile(_rope_table(jnp.arange(t_pad)), (batch, 1))
    cs_s = jnp.repeat(_rope_table(n_pages * PAGE_SIZE + jnp.arange(dec_seq)), dec_batch, axis=0)

    outs = [[] for _ in range(6)]
    for l in range(depth):
        w = _layer_weights(l, p)
        xs, *w16 = _ffn_cast(xs, w["ffn1_norm"], ffn1_w_gate, ffn1_w_up, ffn1_w_down, layer=l)
        xp = _ffn(xp, w["ffn1_norm"], *w16, tm=FFN_ROWS)

        u_p, ckv_p, kpe_p, q_p, k_p, v_p = _inproj(xp, cs_p, w_in_t, w, layer=l)
        attn_p = _flash(q_p, k_p, v_p, batch=batch, t_pad=t_pad)
        xp = _merge_prompt(xp, u_p, attn_p, w, batch=batch, t_pad=t_pad, tm=ATT_BLOCK)
        outs[0].append(ckv_p.reshape(batch, t_pad, KV_LORA)[:, :t_real])
        outs[1].append(kpe_p.reshape(batch, t_pad, 2 * ROPE_DIM)[:, :t_real, :ROPE_DIM])
        outs[2].append(u_p.reshape(batch, t_pad, POOL_WIDTH)[:, t_real - POOL_STATE:t_real])

        u_s, ckv_s, kpe_s, q_s, k_s, _ = _inproj(xs, cs_s, w_in_t, w, layer=l)
        qa, qr, sn = _qabs(q_s, k_s, w, dec_seq=dec_seq)

        def per_seq(t):
            return t.reshape(N_HEADS, dec_seq, dec_batch, -1).transpose(2, 1, 0, 3) \
                    .reshape(dec_batch, dec_seq * N_HEADS, -1)

        c_new = ckv_s.reshape(dec_seq, dec_batch, KV_LORA).transpose(1, 0, 2)
        p_new = kpe_s[:, :ROPE_DIM].reshape(dec_seq, dec_batch, ROPE_DIM).transpose(1, 0, 2)
        n_fill = PAGE_SIZE - dec_seq
        sn = sn.reshape(N_HEADS, dec_seq, dec_seq, dec_batch).transpose(3, 1, 0, 2) \
               .reshape(dec_batch, dec_seq * N_HEADS, dec_seq)
        pc = _sattn(page_table, per_seq(qa), per_seq(qr[..., :ROPE_DIM]),
                    jnp.pad(sn, ((0, 0), (0, 0), (0, n_fill)), constant_values=NEG_INF),
                    jnp.pad(c_new, ((0, 0), (0, n_fill), (0, 0))).astype(BF16),
                    w["wkt"], cache_ckv, cache_kpe_t, layer=l)
        pc = pc.reshape(dec_batch, dec_seq, N_HEADS, KV_LORA).transpose(2, 1, 0, 3) \
               .reshape(N_HEADS, n_s, KV_LORA)
        us = u_s.reshape(dec_seq, dec_batch, POOL_WIDTH)
        xs = _merge_sample(xs, state_t, us, pc, w, layer=l)
        outs[3].append(c_new)
        outs[4].append(p_new)
        outs[5].append(jnp.concatenate([state_t[l, dec_seq:], us], axis=0).transpose(1, 0, 2))

        xs, *w16 = _ffn_cast(xs, w["ffn2_norm"], ffn2_w_gate, ffn2_w_up, ffn2_w_down, layer=l)
        xp = _ffn(xp, w["ffn2_norm"], *w16, tm=FFN_ROWS)

    y_prompt = xp.reshape(batch, t_pad, D_MODEL)[:, N_META:t_real]
    y_sample = xs.reshape(dec_seq, dec_batch, D_MODEL).transpose(1, 0, 2)
    return (y_prompt, y_sample) + tuple(jnp.stack(o) for o in outs)
```

```python
import functools

import jax
import jax.numpy as jnp
from jax import lax
from jax.experimental import pallas as pl
from jax.experimental.pallas import tpu as pltpu

F32 = jnp.float32
BF16 = jnp.bfloat16

D_MODEL = 2048
N_META = 16
POOL_WIDTH = 1024
POOL_WINDOWS = (2, 4, 8, 16)
POOL_GROUP_DIM = 256
POOL_STATE = 15
N_HEADS = 8
NOPE_DIM = 128
ROPE_DIM = 64
ROPE_HALF = 32
QK_DIM = 192
QK_PAD = 256
V_DIM = 128
ATTN_WIDTH = 1024
Q_LORA = 512
KV_LORA = 256
D_FF = 5632
ROPE_THETA = 10000.0
EPS = 1e-6
ATTN_SCALE = QK_DIM ** -0.5
LOG2_E = 1.4426950408889634
NEG_INF = -1e30
PAGE_SIZE = 128

IN_EXT = POOL_WIDTH + Q_LORA + KV_LORA + 2 * ROPE_DIM
FF_TILE = 512
FF_CAST_TILE = 512
FFN_ROWS = 544
ATT_BLOCK = 256
FLASH_HEADS = 4
HALO = 16
KEY_CHUNK = 256
CHUNK_GROUP = 8
VMEM_LIMIT = 56 * 1024 * 1024


def _rms(x, g):
    return x * lax.rsqrt(jnp.mean(x * x, axis=-1, keepdims=True) + EPS) * g


def _dot(a, b):
    return jnp.dot(a, b, preferred_element_type=F32)


def _dot_nt(a, b):
    return lax.dot_general(a, b, (((1,), (1,)), ((), ())), preferred_element_type=F32)


def _params(sem):
    return pltpu.CompilerParams(dimension_semantics=sem, vmem_limit_bytes=VMEM_LIMIT)


def _ffn_init(x_ref, g_ref, o_ref, h_ref):
    @pl.when(pl.program_id(1) == 0)
    def _():
        x = x_ref[...]
        h_ref[...] = _rms(x, g_ref[...]).astype(BF16)
        o_ref[...] = x


def _ffn_kernel(x_ref, g_ref, wg_ref, wu_ref, wd_ref, o_ref, h_ref):
    _ffn_init(x_ref, g_ref, o_ref, h_ref)
    h = h_ref[...]
    gate = _dot(h, wg_ref[...])
    up = _dot(h, wu_ref[...])
    a = (0.5 * gate * jax.nn.sigmoid(gate) * up).astype(BF16)
    o_ref[...] += _dot(a, wd_ref[...])


def _ffn_cast_kernel(x_ref, g_ref, wg_ref, wu_ref, wd_ref, o_ref, wg16_ref, wu16_ref, wd16_ref, h_ref):
    _ffn_init(x_ref, g_ref, o_ref, h_ref)
    h = h_ref[...]
    wg = wg_ref[...].astype(BF16)
    wg16_ref[...] = wg
    gate = _dot(h, wg)
    wu = wu_ref[...].astype(BF16)
    wu16_ref[...] = wu
    up = _dot(h, wu)
    a = (0.5 * gate * jax.nn.sigmoid(gate) * up).astype(BF16)
    wd = wd_ref[...].astype(BF16)
    wd16_ref[...] = wd
    o_ref[...] += _dot(a, wd)


def _ffn(x, g, wg, wu, wd, *, tm):
    n = x.shape[0]
    return pl.pallas_call(
        _ffn_kernel,
        out_shape=jax.ShapeDtypeStruct((n, D_MODEL), F32),
        grid=(n // tm, D_FF // FF_TILE),
        in_specs=[
            pl.BlockSpec((tm, D_MODEL), lambda i, j: (i, 0)),
            pl.BlockSpec((1, D_MODEL), lambda i, j: (0, 0)),
            pl.BlockSpec((D_MODEL, FF_TILE), lambda i, j: (0, j)),
            pl.BlockSpec((D_MODEL, FF_TILE), lambda i, j: (0, j)),
            pl.BlockSpec((FF_TILE, D_MODEL), lambda i, j: (j, 0)),
        ],
        out_specs=pl.BlockSpec((tm, D_MODEL), lambda i, j: (i, 0)),
        scratch_shapes=[pltpu.VMEM((tm, D_MODEL), BF16)],
        compiler_params=_params(("parallel", "arbitrary")),
        name="ffn",
    )(x, g, wg, wu, wd)


def _ffn_cast(x, g, wg, wu, wd, *, layer):
    n = x.shape[0]
    tf = FF_CAST_TILE
    w_in_spec = pl.BlockSpec((None, D_MODEL, tf), lambda i, j: (layer, 0, j))
    w_out_spec = pl.BlockSpec((D_MODEL, tf), lambda i, j: (0, j))
    return pl.pallas_call(
        _ffn_cast_kernel,
        out_shape=(jax.ShapeDtypeStruct((n, D_MODEL), F32),
                   jax.ShapeDtypeStruct((D_MODEL, D_FF), BF16),
                   jax.ShapeDtypeStruct((D_MODEL, D_FF), BF16),
                   jax.ShapeDtypeStruct((D_FF, D_MODEL), BF16)),
        grid=(1, D_FF // tf),
        in_specs=[
            pl.BlockSpec((n, D_MODEL), lambda i, j: (0, 0), pipeline_mode=pl.Buffered(1)),
            pl.BlockSpec((1, D_MODEL), lambda i, j: (0, 0)),
            w_in_spec,
            w_in_spec,
            pl.BlockSpec((None, tf, D_MODEL), lambda i, j: (layer, j, 0)),
        ],
        out_specs=(pl.BlockSpec((n, D_MODEL), lambda i, j: (0, 0)),
                   w_out_spec,
                   w_out_spec,
                   pl.BlockSpec((tf, D_MODEL), lambda i, j: (j, 0))),
        scratch_shapes=[pltpu.VMEM((n, D_MODEL), BF16)],
        compiler_params=_params(("arbitrary", "arbitrary")),
        name="ffn_cast",
    )(x, g, wg, wu, wd)


def _inproj_kernel(x_ref, gmix_ref, win_ref, gqa_ref, wq_ref, gkva_ref, wk_ref, wv_ref,
                   cs_ref, gq_ref, gk_ref,
                   u_ref, ckv_ref, kpe_ref, q_ref, k_ref, v_ref):
    h = _rms(x_ref[...], gmix_ref[...]).astype(BF16)
    z = _dot_nt(h, win_ref[...])
    u_ref[...] = z[:, :POOL_WIDTH]
    ql = _rms(z[:, POOL_WIDTH:POOL_WIDTH + Q_LORA], gqa_ref[...]).astype(BF16)
    o_kv = POOL_WIDTH + Q_LORA
    c = _rms(z[:, o_kv:o_kv + KV_LORA], gkva_ref[...])
    ckv_ref[...] = c

    cs = cs_ref[...]
    lane = lax.broadcasted_iota(jnp.int32, cs.shape, 1)
    low = lane < ROPE_DIM

    def rope(t):
        t = t * cs
        return jnp.where(low, t + pltpu.roll(t, ROPE_DIM, axis=1), 0.0)

    kpe = rope(z[:, o_kv + KV_LORA:])
    kpe_ref[...] = kpe
    kpe_ss = jnp.sum(kpe * kpe, axis=-1, keepdims=True)

    q = _dot(ql, wq_ref[...])
    cb = c.astype(BF16)
    kn = _dot(cb, wk_ref[...])
    vt = _dot_nt(wv_ref[...], cb)
    gq = gq_ref[...]
    gk = gk_ref[...]
    inv_d = 1.0 / QK_DIM
    for hd in range(N_HEADS):
        qa = q[:, hd * QK_PAD:hd * QK_PAD + NOPE_DIM]
        qb = rope(q[:, hd * QK_PAD + NOPE_DIM:(hd + 1) * QK_PAD])
        rs = lax.rsqrt(jnp.sum(qa * qa + qb * qb, axis=-1, keepdims=True) * inv_d + EPS)
        q_ref[hd, :, :NOPE_DIM] = (qa * rs * gq[:, :NOPE_DIM]).astype(BF16)
        q_ref[hd, :, NOPE_DIM:] = (qb * rs * gq[:, NOPE_DIM:]).astype(BF16)
        ka = kn[:, hd * NOPE_DIM:(hd + 1) * NOPE_DIM]
        rk = lax.rsqrt((jnp.sum(ka * ka, axis=-1, keepdims=True) + kpe_ss) * inv_d + EPS)
        k_ref[hd, :, :NOPE_DIM] = (ka * rk * gk[:, :NOPE_DIM]).astype(BF16)
        k_ref[hd, :, NOPE_DIM:] = (kpe * rk * gk[:, NOPE_DIM:]).astype(BF16)
        v_ref[hd] = vt[hd * V_DIM:(hd + 1) * V_DIM, :].astype(BF16)


def _inproj(x, cs, w_in_t, w, *, layer):
    n = x.shape[0]
    tm = ATT_BLOCK
    row = lambda i: (i, 0)
    fix = lambda i: (0, 0)
    head = lambda i: (0, i, 0)
    return pl.pallas_call(
        _inproj_kernel,
        out_shape=(
            jax.ShapeDtypeStruct((n, POOL_WIDTH), F32),
            jax.ShapeDtypeStruct((n, KV_LORA), F32),
            jax.ShapeDtypeStruct((n, 2 * ROPE_DIM), F32),
            jax.ShapeDtypeStruct((N_HEADS, n, QK_PAD), BF16),
            jax.ShapeDtypeStruct((N_HEADS, n, QK_PAD), BF16),
            jax.ShapeDtypeStruct((N_HEADS, n // tm, V_DIM, tm), BF16),
        ),
        grid=(n // tm,),
        in_specs=[
            pl.BlockSpec((tm, D_MODEL), row),
            pl.BlockSpec((1, D_MODEL), fix),
            pl.BlockSpec((None, IN_EXT, D_MODEL), lambda i: (layer, 0, 0)),
            pl.BlockSpec((1, Q_LORA), fix),
            pl.BlockSpec((Q_LORA, N_HEADS * QK_PAD), fix),
            pl.BlockSpec((1, KV_LORA), fix),
            pl.BlockSpec((KV_LORA, N_HEADS * NOPE_DIM), fix),
            pl.BlockSpec((N_HEADS * V_DIM, KV_LORA), fix),
            pl.BlockSpec((tm, 2 * ROPE_DIM), row),
            pl.BlockSpec((1, QK_PAD), fix),
            pl.BlockSpec((1, QK_PAD), fix),
        ],
        out_specs=(
            pl.BlockSpec((tm, POOL_WIDTH), row),
            pl.BlockSpec((tm, KV_LORA), row),
            pl.BlockSpec((tm, 2 * ROPE_DIM), row),
            pl.BlockSpec((N_HEADS, tm, QK_PAD), head),
            pl.BlockSpec((N_HEADS, tm, QK_PAD), head),
            pl.BlockSpec((N_HEADS, None, V_DIM, tm), lambda i: (0, i, 0, 0)),
        ),
        compiler_params=_params(("parallel",)),
        name="inproj",
    )(x, w["mix_norm"], w_in_t, w["q_a_norm"], w["wq"], w["kv_a_norm"], w["wk"], w["wvt"],
      cs, w["gq"], w["gk"])


def _flash_kernel(q_ref, k_ref, v_ref, o_ref):
    i = pl.program_id(2)
    nh, bq, _ = q_ref.shape

    def scores(hd, j):
        off = pl.multiple_of(j * ATT_BLOCK, ATT_BLOCK)
        return _dot_nt(k_ref[hd, pl.ds(off, ATT_BLOCK), :], q_ref[hd])

    def update(hd, j, st, stats):
        m, l, acc = stats
        m_new = jnp.maximum(m, jnp.max(st, axis=0, keepdims=True))
        p = jnp.exp2(st - m_new)
        alpha = jnp.exp2(m - m_new)
        l = alpha * l + jnp.sum(p, axis=0, keepdims=True)
        acc = alpha * acc + _dot(v_ref[hd, j], p.astype(BF16))
        return m_new, l, acc

    def body(j, carry):
        return tuple((scores(hd, j + 1), update(hd, j, *carry[hd])) for hd in range(nh))

    init = (jnp.full((1, bq), -jnp.inf, F32), jnp.zeros((1, bq), F32), jnp.zeros((V_DIM, bq), F32))
    carry = lax.fori_loop(0, i, body, tuple((scores(hd, 0), init) for hd in range(nh)))
    key = lax.broadcasted_iota(jnp.int32, (ATT_BLOCK, bq), 0)
    qry = lax.broadcasted_iota(jnp.int32, (ATT_BLOCK, bq), 1)
    for hd in range(nh):
        st, stats = carry[hd]
        _, l, acc = update(hd, i, jnp.where(key <= qry, st, NEG_INF), stats)
        o_ref[hd] = acc / l


def _flash(q, k, v, *, batch, t_pad):
    nq = t_pad // ATT_BLOCK
    qmap = lambda b, h, i: (h, b * nq + i, 0)
    return pl.pallas_call(
        _flash_kernel,
        out_shape=jax.ShapeDtypeStruct(v.shape, F32),
        grid=(batch, N_HEADS // FLASH_HEADS, nq),
        in_specs=[
            pl.BlockSpec((FLASH_HEADS, ATT_BLOCK, QK_PAD), qmap),
            pl.BlockSpec((FLASH_HEADS, t_pad, QK_PAD), lambda b, h, i: (h, b, 0)),
            pl.BlockSpec((FLASH_HEADS, nq, V_DIM, ATT_BLOCK), lambda b, h, i: (h, b, 0, 0)),
        ],
        out_specs=pl.BlockSpec((FLASH_HEADS, None, V_DIM, ATT_BLOCK),
                               lambda b, h, i: (h, b * nq + i, 0, 0)),
        compiler_params=_params(("parallel", "parallel", "arbitrary")),
        name="flash",
    )(q, k, v)


def _merge_tail(x, d, a, pw_ref, ps_ref, pn_ref, an_ref, wo_ref):
    pool = jnp.concatenate(
        [_dot(d[:, g * POOL_GROUP_DIM:(g + 1) * POOL_GROUP_DIM].astype(BF16), pw_ref[g])
         for g in range(len(POOL_WINDOWS))], axis=-1) * ps_ref[...]
    cat = jnp.concatenate([_rms(pool, pn_ref[...]).astype(BF16),
                           _rms(a, an_ref[...]).astype(BF16)], axis=-1)
    return x + _dot(cat, wo_ref[...])


def _merge_prompt_kernel(x_ref, u_ref, halo_ref, a_ref, pw_ref, ps_ref, pn_ref, an_ref, wo_ref,
                         o_ref, ext_ref):
    i = pl.program_id(1)
    tm = u_ref.shape[0]
    ext_ref[0:HALO, :] = jnp.where(i > 0, halo_ref[...], 0.0)
    ext_ref[HALO:HALO + tm, :] = u_ref[...]
    pos = i * tm + lax.broadcasted_iota(jnp.int32, (tm, 1), 0)
    ds = []
    for g, w in enumerate(POOL_WINDOWS):
        sl = slice(g * POOL_GROUP_DIM, (g + 1) * POOL_GROUP_DIM)
        tok = ext_ref[HALO:HALO + tm, sl]
        acc = tok
        for k in range(1, w):
            acc = acc + ext_ref[HALO - k:HALO - k + tm, sl]
        cnt = jnp.minimum(pos + 1, w).astype(F32)
        ds.append(acc / cnt - tok)
    d = jnp.concatenate(ds, axis=-1)
    a = jnp.concatenate([a_ref[hd] for hd in range(N_HEADS)], axis=0).T
    o_ref[...] = _merge_tail(x_ref[...], d, a, pw_ref, ps_ref, pn_ref, an_ref, wo_ref)


def _merge_weight_specs(fix2, fix3):
    return [
        pl.BlockSpec((len(POOL_WINDOWS), POOL_GROUP_DIM, POOL_GROUP_DIM), fix3),
        pl.BlockSpec((1, POOL_WIDTH), fix2),
        pl.BlockSpec((1, POOL_WIDTH), fix2),
        pl.BlockSpec((1, ATTN_WIDTH), fix2),
        pl.BlockSpec((D_MODEL, D_MODEL), fix2),
    ]


def _merge_prompt(x, u, attn, w, *, batch, t_pad, tm):
    n = x.shape[0]
    nt = t_pad // tm
    row = lambda b, i: (b * nt + i, 0)
    return pl.pallas_call(
        _merge_prompt_kernel,
        out_shape=jax.ShapeDtypeStruct((n, D_MODEL), F32),
        grid=(batch, nt),
        in_specs=[
            pl.BlockSpec((tm, D_MODEL), row),
            pl.BlockSpec((tm, POOL_WIDTH), row),
            pl.BlockSpec((HALO, POOL_WIDTH),
                         lambda b, i: (jnp.maximum((b * nt + i) * (tm // HALO) - 1, 0), 0)),
            pl.BlockSpec((N_HEADS, None, V_DIM, tm), lambda b, i: (0, b * nt + i, 0, 0)),
        ] + _merge_weight_specs(lambda b, i: (0, 0), lambda b, i: (0, 0, 0)),
        out_specs=pl.BlockSpec((tm, D_MODEL), row),
        scratch_shapes=[pltpu.VMEM((HALO + tm, POOL_WIDTH), F32)],
        compiler_params=_params(("parallel", "arbitrary")),
        name="merge_prompt",
    )(x, u, u, attn, w["pool_w"], w["pool_scale"], w["pool_out_norm"], w["attn_out_norm"], w["w_out"])


def _merge_sample_kernel(x_ref, st_ref, us_ref, pc_ref, wv_ref, pw_ref, ps_ref, pn_ref, an_ref, wo_ref,
                         o_ref):
    def ext_row(j, sl):
        return st_ref[j, :, sl] if j < POOL_STATE else us_ref[j - POOL_STATE, :, sl]

    rows = []
    for s in range(us_ref.shape[0]):
        ds = []
        for g, w in enumerate(POOL_WINDOWS):
            sl = slice(g * POOL_GROUP_DIM, (g + 1) * POOL_GROUP_DIM)
            tok = ext_row(POOL_STATE + s, sl)
            acc = tok
            for k in range(1, w):
                acc = acc + ext_row(POOL_STATE + s - k, sl)
            ds.append(acc * (1.0 / w) - tok)
        rows.append(jnp.concatenate(ds, axis=-1))
    d = jnp.concatenate(rows, axis=0)
    a = jnp.concatenate([_dot(pc_ref[hd].astype(BF16), wv_ref[hd]) for hd in range(N_HEADS)], axis=-1)
    o_ref[...] = _merge_tail(x_ref[...], d, a, pw_ref, ps_ref, pn_ref, an_ref, wo_ref)


def _merge_sample(x, state_t, us, pc, w, *, layer):
    n = x.shape[0]
    dec_seq, dec_batch, _ = us.shape
    once = pl.Buffered(1)
    return pl.pallas_call(
        _merge_sample_kernel,
        out_shape=jax.ShapeDtypeStruct((n, D_MODEL), F32),
        grid=(1,),
        in_specs=[
            pl.BlockSpec((n, D_MODEL), lambda s: (0, 0), pipeline_mode=once),
            pl.BlockSpec((None, POOL_STATE, dec_batch, POOL_WIDTH), lambda s: (layer, 0, 0, 0),
                         pipeline_mode=once),
            pl.BlockSpec((dec_seq, dec_batch, POOL_WIDTH), lambda s: (0, 0, 0), pipeline_mode=once),
            pl.BlockSpec((N_HEADS, n, KV_LORA), lambda s: (0, 0, 0), pipeline_mode=once),
            pl.BlockSpec((N_HEADS, KV_LORA, V_DIM), lambda s: (0, 0, 0), pipeline_mode=once),
            pl.BlockSpec((len(POOL_WINDOWS), POOL_GROUP_DIM, POOL_GROUP_DIM), lambda s: (0, 0, 0),
                         pipeline_mode=once),
            pl.BlockSpec((1, POOL_WIDTH), lambda s: (0, 0)),
            pl.BlockSpec((1, POOL_WIDTH), lambda s: (0, 0)),
            pl.BlockSpec((1, ATTN_WIDTH), lambda s: (0, 0)),
            pl.BlockSpec((D_MODEL, D_MODEL), lambda s: (0, 0), pipeline_mode=once),
        ],
        out_specs=pl.BlockSpec((n, D_MODEL), lambda s: (0, 0)),
        compiler_params=_params(("arbitrary",)),
        name="merge_sample",
    )(x, state_t, us, pc, w["wv3"], w["pool_w"], w["pool_scale"], w["pool_out_norm"],
      w["attn_out_norm"], w["w_out"])


def _qabs_kernel(q_ref, k_ref, wk_ref, gk_ref, qa_ref, qr_ref, sn_ref, *, dec_seq):
    qb = q_ref[...]
    q = qb.astype(F32)
    gk = gk_ref[...]
    qn = (q[:, :NOPE_DIM] * gk[:, :NOPE_DIM]).astype(BF16)
    qa_ref[...] = _dot_nt(qn, wk_ref[...]).astype(BF16)
    qr_ref[...] = (q[:, NOPE_DIM:] * gk[:, NOPE_DIM:]).astype(BF16)
    nb = qb.shape[0] // dec_seq
    eye = (lax.broadcasted_iota(jnp.int32, (nb, nb), 0) == lax.broadcasted_iota(jnp.int32, (nb, nb), 1))
    for s in range(dec_seq):
        for j in range(dec_seq):
            r = s * dec_seq + j
            if j <= s:
                mm = _dot_nt(qb[s * nb:(s + 1) * nb], k_ref[j * nb:(j + 1) * nb, :])
                sn_ref[r:r + 1, :] = jnp.sum(jnp.where(eye, mm, 0.0), axis=0, keepdims=True)
            else:
                sn_ref[r:r + 1, :] = jnp.full((1, nb), NEG_INF, F32)


def _qabs(q, k, w, *, dec_seq):
    n = q.shape[1]
    head = lambda h: (h, 0, 0)
    return pl.pallas_call(
        functools.partial(_qabs_kernel, dec_seq=dec_seq),
        out_shape=(jax.ShapeDtypeStruct((N_HEADS, n, KV_LORA), BF16),
                   jax.ShapeDtypeStruct((N_HEADS, n, QK_PAD - NOPE_DIM), BF16),
                   jax.ShapeDtypeStruct((N_HEADS, dec_seq * dec_seq, n // dec_seq), F32)),
        grid=(N_HEADS,),
        in_specs=[
            pl.BlockSpec((None, n, QK_PAD), head),
            pl.BlockSpec((None, n, QK_PAD), head),
            pl.BlockSpec((None, KV_LORA, NOPE_DIM), head),
            pl.BlockSpec((1, QK_PAD), lambda h: (0, 0)),
        ],
        out_specs=(pl.BlockSpec((None, n, KV_LORA), head),
                   pl.BlockSpec((None, n, QK_PAD - NOPE_DIM), head),
                   pl.BlockSpec((None, dec_seq * dec_seq, n // dec_seq), head)),
        compiler_params=_params(("parallel",)),
        name="qabs",
    )(q, k, w["wk3"], w["gk"])


def _sattn_kernel(pt_ref, qa_ref, qr_ref, snew_ref, cnew_ref, wkt_ref, ckv_hbm, kpe_hbm, o_ref,
                  a_ref, cbuf, pbuf, cb16, s_all, sem, *, layer, n_pages, n_rows):
    b = pl.program_id(0)
    nb = pl.num_programs(0)
    slot = b % 2
    n_kn = N_HEADS * NOPE_DIM

    def page_copies(page, p, sl):
        return (pltpu.make_async_copy(ckv_hbm.at[layer, page], cbuf.at[sl, p], sem.at[0, sl]),
                pltpu.make_async_copy(kpe_hbm.at[layer, page], pbuf.at[sl, p], sem.at[1, sl]))

    def issue(seq, sl):
        def body(p, carry):
            for cp in page_copies(pt_ref[seq, p], p, sl):
                cp.start()
            return carry
        lax.fori_loop(0, n_pages, body, 0, unroll=8)

    def wait(sl):
        for p in range(n_pages):
            for cp in page_copies(0, p, sl):
                cp.wait()

    @pl.when(b == 0)
    def _():
        a_ref[0:n_kn, :] = wkt_ref[...]
        issue(0, 0)

    @pl.when(b + 1 < nb)
    def _():
        issue(b + 1, 1 - slot)

    a_ref[n_kn:n_kn + n_rows, :] = qa_ref[...]
    qr = qr_ref[...]
    n_chunks = n_pages * PAGE_SIZE // KEY_CHUNK
    pages_per_chunk = KEY_CHUNK // PAGE_SIZE

    def scores(cb, pt):
        nk = cb.shape[0]
        r_all = _dot_nt(a_ref[...], cb)
        kn = r_all[:n_kn]
        ssq = jnp.sum((kn * kn).reshape(N_HEADS, NOPE_DIM, nk), axis=1)
        pe2 = jnp.sum(pt * pt, axis=0, keepdims=True)
        r = lax.rsqrt((ssq + pe2) * (1.0 / QK_DIM) + EPS)
        raw_r = _dot(qr, pt.astype(BF16))
        return (r_all[n_kn:] + raw_r) * jnp.concatenate([r] * (n_rows // N_HEADS), axis=0)

    def score_group(g):
        for j in range(CHUNK_GROUP):
            c = g * CHUNK_GROUP + j
            cb = cbuf[slot, pl.ds(c * pages_per_chunk, pages_per_chunk)] \
                .reshape(KEY_CHUNK, KV_LORA).astype(BF16)
            cb16[pl.ds(pl.multiple_of(c * KEY_CHUNK, KEY_CHUNK), KEY_CHUNK), :] = cb
            pt = jnp.concatenate([pbuf[slot, c * pages_per_chunk + k] for k in range(pages_per_chunk)],
                                 axis=1)
            s_all[c] = scores(cb, pt)

    def accumulate(s, c_rows, carry):
        m, l, acc = carry
        m_new = jnp.maximum(m, jnp.max(s, axis=-1, keepdims=True))
        p = jnp.exp2(s - m_new)
        alpha = jnp.exp2(m - m_new)
        l = alpha * l + jnp.sum(p, axis=-1, keepdims=True)
        acc = alpha * acc + _dot(p.astype(BF16), c_rows)
        return m_new, l, acc

    def accumulate_group(g, carry):
        s = jnp.concatenate([s_all[g * CHUNK_GROUP + j] for j in range(CHUNK_GROUP)], axis=1)
        n_g = CHUNK_GROUP * KEY_CHUNK
        return accumulate(s, cb16[pl.ds(pl.multiple_of(g * n_g, n_g), n_g), :], carry)

    wait(slot)
    n_groups = n_chunks // CHUNK_GROUP
    score_group(0)

    def body(g, carry):
        carry = accumulate_group(g - 1, carry)
        score_group(g)
        return carry

    carry = (jnp.full((n_rows, 1), -jnp.inf, F32), jnp.zeros((n_rows, 1), F32),
             jnp.zeros((n_rows, KV_LORA), F32))
    carry = lax.fori_loop(1, n_groups, body, carry)
    carry = accumulate_group(n_groups - 1, carry)
    _, l, acc = accumulate(snew_ref[...], cnew_ref[...], carry)
    o_ref[...] = acc / l


def _sattn(page_table, qa, qr, snew, cnew, wkt, cache_ckv, cache_kpe_t, *, layer):
    nb, n_pages = page_table.shape
    n_rows = qa.shape[1]
    n_keys = n_pages * PAGE_SIZE
    n_chunks = n_keys // KEY_CHUNK
    assert n_chunks % CHUNK_GROUP == 0 and KEY_CHUNK % PAGE_SIZE == 0
    seq3 = lambda b, pt: (b, 0, 0)
    grid_spec = pltpu.PrefetchScalarGridSpec(
        num_scalar_prefetch=1,
        grid=(nb,),
        in_specs=[
            pl.BlockSpec((None, n_rows, KV_LORA), seq3),
            pl.BlockSpec((None, n_rows, ROPE_DIM), seq3),
            pl.BlockSpec((None, n_rows, PAGE_SIZE), seq3),
            pl.BlockSpec((None, PAGE_SIZE, KV_LORA), seq3),
            pl.BlockSpec((N_HEADS * NOPE_DIM, KV_LORA), lambda b, pt: (0, 0)),
            pl.BlockSpec(memory_space=pl.ANY),
            pl.BlockSpec(memory_space=pl.ANY),
        ],
        out_specs=pl.BlockSpec((None, n_rows, KV_LORA), seq3),
        scratch_shapes=[
            pltpu.VMEM((N_HEADS * NOPE_DIM + n_rows, KV_LORA), BF16),
            pltpu.VMEM((2, n_pages, PAGE_SIZE, KV_LORA), F32),
            pltpu.VMEM((2, n_pages, ROPE_DIM, PAGE_SIZE), F32),
            pltpu.VMEM((n_keys, KV_LORA), BF16),
            pltpu.VMEM((n_chunks, n_rows, KEY_CHUNK), F32),
            pltpu.SemaphoreType.DMA((2, 2)),
        ],
    )
    return pl.pallas_call(
        functools.partial(_sattn_kernel, layer=layer, n_pages=n_pages, n_rows=n_rows),
        out_shape=jax.ShapeDtypeStruct((nb, n_rows, KV_LORA), F32),
        grid_spec=grid_spec,
        compiler_params=_params(("arbitrary",)),
        name="sattn",
    )(page_table, qa, qr, snew, cnew, wkt, cache_ckv, cache_kpe_t)


def _rot_half_cols(w):
    return jnp.concatenate([-w[..., ROPE_HALF:], w[..., :ROPE_HALF]], axis=-1)


def _rope_table(pos):
    inv = ROPE_THETA ** (-jnp.arange(ROPE_HALF, dtype=F32) / ROPE_HALF)
    ang = pos.astype(F32)[:, None] * inv[None, :]
    c, s = jnp.cos(ang), jnp.sin(ang)
    return jnp.concatenate([c, c, s, s], axis=-1)


def _layer_weights(l, p):
    row = lambda v: v[l][None, :].astype(F32)
    wq = p["w_q_b"][l].reshape(Q_LORA, N_HEADS, QK_DIM)
    wq_rope = wq[..., NOPE_DIM:]
    wkv = p["w_kv_b"][l].reshape(KV_LORA, N_HEADS, NOPE_DIM + V_DIM)
    wk = wkv[..., :NOPE_DIM]
    wv = wkv[..., NOPE_DIM:]
    zeros = jnp.zeros((QK_PAD - QK_DIM,), F32)

    def head_gain(g_nope, g_rope):
        return jnp.concatenate([g_nope[l], g_rope[l], g_rope[l], zeros])[None, :]

    out = {
        "mix_norm": row(p["mix_norm"]),
        "q_a_norm": row(p["q_a_norm"]),
        "wq": jnp.concatenate([wq, _rot_half_cols(wq_rope)], axis=-1)
              .reshape(Q_LORA, N_HEADS * QK_PAD).astype(BF16),
        "kv_a_norm": row(p["kv_a_norm"]),
        "wk": wk.reshape(KV_LORA, N_HEADS * NOPE_DIM).astype(BF16),
        "wvt": wv.reshape(KV_LORA, N_HEADS * V_DIM).T.astype(BF16),
        "wk3": wk.transpose(1, 0, 2).astype(BF16),
        "wkt": wk.reshape(KV_LORA, N_HEADS * NOPE_DIM).T.astype(BF16),
        "wv3": wv.transpose(1, 0, 2).astype(BF16),
        "gq": head_gain(p["q_norm_nope"], p["q_norm_rope"]) * (ATTN_SCALE * LOG2_E),
        "gk": head_gain(p["k_norm_nope"], p["k_norm_rope"]),
        "pool_w": p["pool_w"][l].astype(BF16),
        "pool_scale": row(p["pool_scale"]),
        "pool_out_norm": row(p["pool_out_norm"]),
        "attn_out_norm": row(p["attn_out_norm"]),
        "w_out": p["w_out"][l].astype(BF16),
        "ffn1_norm": row(p["ffn1_norm"]),
        "ffn2_norm": row(p["ffn2_norm"]),
    }
    return out


def kernel(x_prompt, x_sample, cache_ckv, cache_kpe, state_pool, page_table, meta_tokens, ffn1_norm, ffn1_w_gate, ffn1_w_up, ffn1_w_down, mix_norm, w_in, pool_w, pool_scale, q_a_norm, w_q_b, kv_a_norm, w_kv_b, q_norm_nope, q_norm_rope, k_norm_nope, k_norm_rope, pool_out_norm, attn_out_norm, w_out, ffn2_norm, ffn2_w_gate, ffn2_w_up, ffn2_w_down):
    p = dict(ffn1_norm=ffn1_norm, ffn1_w_gate=ffn1_w_gate, ffn1_w_up=ffn1_w_up, ffn1_w_down=ffn1_w_down,
             mix_norm=mix_norm, w_in=w_in, pool_w=pool_w, pool_scale=pool_scale, q_a_norm=q_a_norm,
             w_q_b=w_q_b, kv_a_norm=kv_a_norm, w_kv_b=w_kv_b, q_norm_nope=q_norm_nope,
             q_norm_rope=q_norm_rope, k_norm_nope=k_norm_nope, k_norm_rope=k_norm_rope,
             pool_out_norm=pool_out_norm, attn_out_norm=attn_out_norm, w_out=w_out,
             ffn2_norm=ffn2_norm, ffn2_w_gate=ffn2_w_gate, ffn2_w_up=ffn2_w_up, ffn2_w_down=ffn2_w_down)
    depth = w_in.shape[0]
    batch, seq, _ = x_prompt.shape
    dec_batch, dec_seq, _ = x_sample.shape
    n_pages = page_table.shape[1]
    t_real = N_META + seq
    t_pad = -(-t_real // ATT_BLOCK) * ATT_BLOCK
    n_p = batch * t_pad
    n_s = dec_seq * dec_batch
    assert n_p % FFN_ROWS == 0 and dec_batch % 8 == 0 and dec_seq * N_HEADS % 8 == 0
    cache_kpe_t = jnp.swapaxes(cache_kpe, 2, 3)
    state_t = jnp.swapaxes(state_pool.astype(F32), 1, 2)
    w_in_t = jnp.swapaxes(w_in, 1, 2).astype(BF16)
    w_pe_t = w_in_t[:, POOL_WIDTH + Q_LORA + KV_LORA:]
    w_in_t = jnp.concatenate([w_in_t, -w_pe_t[:, ROPE_HALF:], w_pe_t[:, :ROPE_HALF]], axis=1)

    meta = jnp.broadcast_to(meta_tokens[None].astype(F32), (batch, N_META, D_MODEL))
    xp = jnp.concatenate([meta, x_prompt, jnp.zeros((batch, t_pad - t_real, D_MODEL), F32)], axis=1)
    xp = xp.reshape(n_p, D_MODEL)
    xs = x_sample.transpose(1, 0, 2).reshape(n_s, D_MODEL)
    cs_p = jnp.tile(_rope_table(jnp.arange(t_pad)), (batch, 1))
    cs_s = jnp.repeat(_rope_table(n_pages * PAGE_SIZE + jnp.arange(dec_seq)), dec_batch, axis=0)

    outs = [[] for _ in range(6)]
    for l in range(depth):
        w = _layer_weights(l, p)
        xs, *w16 = _ffn_cast(xs, w["ffn1_norm"], ffn1_w_gate, ffn1_w_up, ffn1_w_down, layer=l)
        xp = _ffn(xp, w["ffn1_norm"], *w16, tm=FFN_ROWS)

        u_p, ckv_p, kpe_p, q_p, k_p, v_p = _inproj(xp, cs_p, w_in_t, w, layer=l)
        attn_p = _flash(q_p, k_p, v_p, batch=batch, t_pad=t_pad)
        xp = _merge_prompt(xp, u_p, attn_p, w, batch=batch, t_pad=t_pad, tm=ATT_BLOCK)
        outs[0].append(ckv_p.reshape(batch, t_pad, KV_LORA)[:, :t_real])
        outs[1].append(kpe_p.reshape(batch, t_pad, 2 * ROPE_DIM)[:, :t_real, :ROPE_DIM])
        outs[2].append(u_p.reshape(batch, t_pad, POOL_WIDTH)[:, t_real - POOL_STATE:t_real])

        u_s, ckv_s, kpe_s, q_s, k_s, _ = _inproj(xs, cs_s, w_in_t, w, layer=l)
        qa, qr, sn = _qabs(q_s, k_s, w, dec_seq=dec_seq)

        def per_seq(t):
            return t.reshape(N_HEADS, dec_seq, dec_batch, -1).transpose(2, 1, 0, 3) \
                    .reshape(dec_batch, dec_seq * N_HEADS, -1)

        c_new = ckv_s.reshape(dec_seq, dec_batch, KV_LORA).transpose(1, 0, 2)
        p_new = kpe_s[:, :ROPE_DIM].reshape(dec_seq, dec_batch, ROPE_DIM).transpose(1, 0, 2)
        n_fill = PAGE_SIZE - dec_seq
        sn = sn.reshape(N_HEADS, dec_seq, dec_seq, dec_batch).transpose(3, 1, 0, 2) \
               .reshape(dec_batch, dec_seq * N_HEADS, dec_seq)
        pc = _sattn(page_table, per_seq(qa), per_seq(qr[..., :ROPE_DIM]),
                    jnp.pad(sn, ((0, 0), (0, 0), (0, n_fill)), constant_values=NEG_INF),
                    jnp.pad(c_new, ((0, 0), (0, n_fill), (0, 0))).astype(BF16),
                    w["wkt"], cache_ckv, cache_kpe_t, layer=l)
        pc = pc.reshape(dec_batch, dec_seq, N_HEADS, KV_LORA).transpose(2, 1, 0, 3) \
               .reshape(N_HEADS, n_s, KV_LORA)
        us = u_s.reshape(dec_seq, dec_batch, POOL_WIDTH)
        xs = _merge_sample(xs, state_t, us, pc, w, layer=l)
        outs[3].append(c_new)
        outs[4].append(p_new)
        outs[5].append(jnp.concatenate([state_t[l, dec_seq:], us], axis=0).transpose(1, 0, 2))

        xs, *w16 = _ffn_cast(xs, w["ffn2_norm"], ffn2_w_gate, ffn2_w_up, ffn2_w_down, layer=l)
        xp = _ffn(xp, w["ffn2_norm"], *w16, tm=FFN_ROWS)

    y_prompt = xp.reshape(batch, t_pad, D_MODEL)[:, N_META:t_real]
    y_sample = xs.reshape(dec_seq, dec_batch, D_MODEL).transpose(1, 0, 2)
    return (y_prompt, y_sample) + tuple(jnp.stack(o) for o in outs)
```

```python
import functools

import jax
import jax.numpy as jnp
from jax import lax
from jax.experimental import pallas as pl
from jax.experimental.pallas import tpu as pltpu

F32 = jnp.float32
BF16 = jnp.bfloat16

D_MODEL = 2048
N_META = 16
POOL_WIDTH = 1024
POOL_WINDOWS = (2, 4, 8, 16)
POOL_GROUP_DIM = 256
POOL_STATE = 15
N_HEADS = 8
NOPE_DIM = 128
ROPE_DIM = 64
ROPE_HALF = 32
QK_DIM = 192
QK_PAD = 256
V_DIM = 128
ATTN_WIDTH = 1024
Q_LORA = 512
KV_LORA = 256
D_FF = 5632
ROPE_THETA = 10000.0
EPS = 1e-6
ATTN_SCALE = QK_DIM ** -0.5
LOG2_E = 1.4426950408889634
NEG_INF = -1e30
PAGE_SIZE = 128

IN_EXT = POOL_WIDTH + Q_LORA + KV_LORA + 2 * ROPE_DIM
FF_TILE = 512
FF_CAST_TILE = 512
FFN_ROWS = 544
ATT_BLOCK = 256
FLASH_HEADS = 4
HALO = 16
KEY_CHUNK = 256
CHUNK_GROUP = 8
VMEM_LIMIT = 56 * 1024 * 1024


def _rms(x, g):
    return x * lax.rsqrt(jnp.mean(x * x, axis=-1, keepdims=True) + EPS) * g


def _dot(a, b):
    return jnp.dot(a, b, preferred_element_type=F32)


def _dot_nt(a, b):
    return lax.dot_general(a, b, (((1,), (1,)), ((), ())), preferred_element_type=F32)


def _params(sem):
    return pltpu.CompilerParams(dimension_semantics=sem, vmem_limit_bytes=VMEM_LIMIT)


def _ffn_init(x_ref, g_ref, o_ref, h_ref):
    @pl.when(pl.program_id(1) == 0)
    def _():
        x = x_ref[...]
        h_ref[...] = _rms(x, g_ref[...]).astype(BF16)
        o_ref[...] = x


def _ffn_kernel(x_ref, g_ref, wg_ref, wu_ref, wd_ref, o_ref, h_ref):
    _ffn_init(x_ref, g_ref, o_ref, h_ref)
    h = h_ref[...]
    gate = _dot(h, wg_ref[...])
    up = _dot(h, wu_ref[...])
    a = (0.5 * gate * jax.nn.sigmoid(gate) * up).astype(BF16)
    o_ref[...] += _dot(a, wd_ref[...])


def _ffn_cast_kernel(x_ref, g_ref, wg_ref, wu_ref, wd_ref, o_ref, wg16_ref, wu16_ref, wd16_ref, h_ref):
    _ffn_init(x_ref, g_ref, o_ref, h_ref)
    h = h_ref[...]
    wg = wg_ref[...].astype(BF16)
    wg16_ref[...] = wg
    gate = _dot(h, wg)
    wu = wu_ref[...].astype(BF16)
    wu16_ref[...] = wu
    up = _dot(h, wu)
    a = (0.5 * gate * jax.nn.sigmoid(gate) * up).astype(BF16)
    wd = wd_ref[...].astype(BF16)
    wd16_ref[...] = wd
    o_ref[...] += _dot(a, wd)


def _ffn(x, g, wg, wu, wd, *, tm):
    n = x.shape[0]
    return pl.pallas_call(
        _ffn_kernel,
        out_shape=jax.ShapeDtypeStruct((n, D_MODEL), F32),
        grid=(n // tm, D_FF // FF_TILE),
        in_specs=[
            pl.BlockSpec((tm, D_MODEL), lambda i, j: (i, 0)),
            pl.BlockSpec((1, D_MODEL), lambda i, j: (0, 0)),
            pl.BlockSpec((D_MODEL, FF_TILE), lambda i, j: (0, j)),
            pl.BlockSpec((D_MODEL, FF_TILE), lambda i, j: (0, j)),
            pl.BlockSpec((FF_TILE, D_MODEL), lambda i, j: (j, 0)),
        ],
        out_specs=pl.BlockSpec((tm, D_MODEL), lambda i, j: (i, 0)),
        scratch_shapes=[pltpu.VMEM((tm, D_MODEL), BF16)],
        compiler_params=_params(("parallel", "arbitrary")),
        name="ffn",
    )(x, g, wg, wu, wd)


def _ffn_cast(x, g, wg, wu, wd, *, layer):
    n = x.shape[0]
    tf = FF_CAST_TILE
    w_in_spec = pl.BlockSpec((None, D_MODEL, tf), lambda i, j: (layer, 0, j))
    w_out_spec = pl.BlockSpec((D_MODEL, tf), lambda i, j: (0, j))
    return pl.pallas_call(
        _ffn_cast_kernel,
        out_shape=(jax.ShapeDtypeStruct((n, D_MODEL), F32),
                   jax.ShapeDtypeStruct((D_MODEL, D_FF), BF16),
                   jax.ShapeDtypeStruct((D_MODEL, D_FF), BF16),
                   jax.ShapeDtypeStruct((D_FF, D_MODEL), BF16)),
        grid=(1, D_FF // tf),
        in_specs=[
            pl.BlockSpec((n, D_MODEL), lambda i, j: (0, 0), pipeline_mode=pl.Buffered(1)),
            pl.BlockSpec((1, D_MODEL), lambda i, j: (0, 0)),
            w_in_spec,
            w_in_spec,
            pl.BlockSpec((None, tf, D_MODEL), lambda i, j: (layer, j, 0)),
        ],
        out_specs=(pl.BlockSpec((n, D_MODEL), lambda i, j: (0, 0)),
                   w_out_spec,
                   w_out_spec,
                   pl.BlockSpec((tf, D_MODEL), lambda i, j: (j, 0))),
        scratch_shapes=[pltpu.VMEM((n, D_MODEL), BF16)],
        compiler_params=_params(("arbitrary", "arbitrary")),
        name="ffn_cast",
    )(x, g, wg, wu, wd)


def _inproj_kernel(x_ref, gmix_ref, win_ref, gqa_ref, wq_ref, gkva_ref, wk_ref, wv_ref,
                   cs_ref, gq_ref, gk_ref,
                   u_ref, ckv_ref, kpe_ref, q_ref, k_ref, v_ref):
    h = _rms(x_ref[...], gmix_ref[...]).astype(BF16)
    z = _dot_nt(h, win_ref[...])
    u_ref[...] = z[:, :POOL_WIDTH]
    ql = _rms(z[:, POOL_WIDTH:POOL_WIDTH + Q_LORA], gqa_ref[...]).astype(BF16)
    o_kv = POOL_WIDTH + Q_LORA
    c = _rms(z[:, o_kv:o_kv + KV_LORA], gkva_ref[...])
    ckv_ref[...] = c

    cs = cs_ref[...]
    lane = lax.broadcasted_iota(jnp.int32, cs.shape, 1)
    low = lane < ROPE_DIM

    def rope(t):
        t = t * cs
        return jnp.where(low, t + pltpu.roll(t, ROPE_DIM, axis=1), 0.0)

    kpe = rope(z[:, o_kv + KV_LORA:])
    kpe_ref[...] = kpe
    kpe_ss = jnp.sum(kpe * kpe, axis=-1, keepdims=True)

    q = _dot(ql, wq_ref[...])
    cb = c.astype(BF16)
    kn = _dot(cb, wk_ref[...])
    vt = _dot_nt(wv_ref[...], cb)
    gq = gq_ref[...]
    gk = gk_ref[...]
    inv_d = 1.0 / QK_DIM
    for hd in range(N_HEADS):
        qa = q[:, hd * QK_PAD:hd * QK_PAD + NOPE_DIM]
        qb = rope(q[:, hd * QK_PAD + NOPE_DIM:(hd + 1) * QK_PAD])
        rs = lax.rsqrt(jnp.sum(qa * qa + qb * qb, axis=-1, keepdims=True) * inv_d + EPS)
        q_ref[hd, :, :NOPE_DIM] = (qa * rs * gq[:, :NOPE_DIM]).astype(BF16)
        q_ref[hd, :, NOPE_DIM:] = (qb * rs * gq[:, NOPE_DIM:]).astype(BF16)
        ka = kn[:, hd * NOPE_DIM:(hd + 1) * NOPE_DIM]
        rk = lax.rsqrt((jnp.sum(ka * ka, axis=-1, keepdims=True) + kpe_ss) * inv_d + EPS)
        k_ref[hd, :, :NOPE_DIM] = (ka * rk * gk[:, :NOPE_DIM]).astype(BF16)
        k_ref[hd, :, NOPE_DIM:] = (kpe * rk * gk[:, NOPE_DIM:]).astype(BF16)
        v_ref[hd] = vt[hd * V_DIM:(hd + 1) * V_DIM, :].astype(BF16)


def _inproj(x, cs, w_in_t, w, *, layer):
    n = x.shape[0]
    tm = ATT_BLOCK
    row = lambda i: (i, 0)
    fix = lambda i: (0, 0)
    head = lambda i: (0, i, 0)
    return pl.pallas_call(
        _inproj_kernel,
        out_shape=(
            jax.ShapeDtypeStruct((n, POOL_WIDTH), F32),
            jax.ShapeDtypeStruct((n, KV_LORA), F32),
            jax.ShapeDtypeStruct((n, 2 * ROPE_DIM), F32),
            jax.ShapeDtypeStruct((N_HEADS, n, QK_PAD), BF16),
            jax.ShapeDtypeStruct((N_HEADS, n, QK_PAD), BF16),
            jax.ShapeDtypeStruct((N_HEADS, n // tm, V_DIM, tm), BF16),
        ),
        grid=(n // tm,),
        in_specs=[
            pl.BlockSpec((tm, D_MODEL), row),
            pl.BlockSpec((1, D_MODEL), fix),
            pl.BlockSpec((None, IN_EXT, D_MODEL), lambda i: (layer, 0, 0)),
            pl.BlockSpec((1, Q_LORA), fix),
            pl.BlockSpec((Q_LORA, N_HEADS * QK_PAD), fix),
            pl.BlockSpec((1, KV_LORA), fix),
            pl.BlockSpec((KV_LORA, N_HEADS * NOPE_DIM), fix),
            pl.BlockSpec((N_HEADS * V_DIM, KV_LORA), fix),
            pl.BlockSpec((tm, 2 * ROPE_DIM), row),
            pl.BlockSpec((1, QK_PAD), fix),
            pl.BlockSpec((1, QK_PAD), fix),
        ],
        out_specs=(
            pl.BlockSpec((tm, POOL_WIDTH), row),
            pl.BlockSpec((tm, KV_LORA), row),
            pl.BlockSpec((tm, 2 * ROPE_DIM), row),
            pl.BlockSpec((N_HEADS, tm, QK_PAD), head),
            pl.BlockSpec((N_HEADS, tm, QK_PAD), head),
            pl.BlockSpec((N_HEADS, None, V_DIM, tm), lambda i: (0, i, 0, 0)),
        ),
        compiler_params=_params(("parallel",)),
        name="inproj",
    )(x, w["mix_norm"], w_in_t, w["q_a_norm"], w["wq"], w["kv_a_norm"], w["wk"], w["wvt"],
      cs, w["gq"], w["gk"])


def _flash_kernel(q_ref, k_ref, v_ref, o_ref):
    i = pl.program_id(2)
    nh, bq, _ = q_ref.shape

    def scores(hd, j):
        off = pl.multiple_of(j * ATT_BLOCK, ATT_BLOCK)
        return _dot_nt(k_ref[hd, pl.ds(off, ATT_BLOCK), :], q_ref[hd])

    def update(hd, j, st, stats):
        m, l, acc = stats
        m_new = jnp.maximum(m, jnp.max(st, axis=0, keepdims=True))
        p = jnp.exp2(st - m_new)
        alpha = jnp.exp2(m - m_new)
        l = alpha * l + jnp.sum(p, axis=0, keepdims=True)
        acc = alpha * acc + _dot(v_ref[hd, j], p.astype(BF16))
        return m_new, l, acc

    def body(j, carry):
        return tuple((scores(hd, j + 1), update(hd, j, *carry[hd])) for hd in range(nh))

    init = (jnp.full((1, bq), -jnp.inf, F32), jnp.zeros((1, bq), F32), jnp.zeros((V_DIM, bq), F32))
    carry = lax.fori_loop(0, i, body, tuple((scores(hd, 0), init) for hd in range(nh)))
    key = lax.broadcasted_iota(jnp.int32, (ATT_BLOCK, bq), 0)
    qry = lax.broadcasted_iota(jnp.int32, (ATT_BLOCK, bq), 1)
    for hd in range(nh):
        st, stats = carry[hd]
        _, l, acc = update(hd, i, jnp.where(key <= qry, st, NEG_INF), stats)
        o_ref[hd] = acc / l


def _flash(q, k, v, *, batch, t_pad):
    nq = t_pad // ATT_BLOCK
    qmap = lambda b, h, i: (h, b * nq + i, 0)
    return pl.pallas_call(
        _flash_kernel,
        out_shape=jax.ShapeDtypeStruct(v.shape, F32),
        grid=(batch, N_HEADS // FLASH_HEADS, nq),
        in_specs=[
            pl.BlockSpec((FLASH_HEADS, ATT_BLOCK, QK_PAD), qmap),
            pl.BlockSpec((FLASH_HEADS, t_pad, QK_PAD), lambda b, h, i: (h, b, 0)),
            pl.BlockSpec((FLASH_HEADS, nq, V_DIM, ATT_BLOCK), lambda b, h, i: (h, b, 0, 0)),
        ],
        out_specs=pl.BlockSpec((FLASH_HEADS, None, V_DIM, ATT_BLOCK),
                               lambda b, h, i: (h, b * nq + i, 0, 0)),
        compiler_params=_params(("parallel", "parallel", "arbitrary")),
        name="flash",
    )(q, k, v)


def _merge_tail(x, d, a, pw_ref, ps_ref, pn_ref, an_ref, wo_ref):
    pool = jnp.concatenate(
        [_dot(d[:, g * POOL_GROUP_DIM:(g + 1) * POOL_GROUP_DIM].astype(BF16), pw_ref[g])
         for g in range(len(POOL_WINDOWS))], axis=-1) * ps_ref[...]
    cat = jnp.concatenate([_rms(pool, pn_ref[...]).astype(BF16),
                           _rms(a, an_ref[...]).astype(BF16)], axis=-1)
    return x + _dot(cat, wo_ref[...])


def _merge_prompt_kernel(x_ref, u_ref, halo_ref, a_ref, pw_ref, ps_ref, pn_ref, an_ref, wo_ref,
                         o_ref, ext_ref):
    i = pl.program_id(1)
    tm = u_ref.shape[0]
    ext_ref[0:HALO, :] = jnp.where(i > 0, halo_ref[...], 0.0)
    ext_ref[HALO:HALO + tm, :] = u_ref[...]
    pos = i * tm + lax.broadcasted_iota(jnp.int32, (tm, 1), 0)
    ds = []
    for g, w in enumerate(POOL_WINDOWS):
        sl = slice(g * POOL_GROUP_DIM, (g + 1) * POOL_GROUP_DIM)
        tok = ext_ref[HALO:HALO + tm, sl]
        acc = tok
        for k in range(1, w):
            acc = acc + ext_ref[HALO - k:HALO - k + tm, sl]
        cnt = jnp.minimum(pos + 1, w).astype(F32)
        ds.append(acc / cnt - tok)
    d = jnp.concatenate(ds, axis=-1)
    a = jnp.concatenate([a_ref[hd] for hd in range(N_HEADS)], axis=0).T
    o_ref[...] = _merge_tail(x_ref[...], d, a, pw_ref, ps_ref, pn_ref, an_ref, wo_ref)


def _merge_weight_specs(fix2, fix3):
    return [
        pl.BlockSpec((len(POOL_WINDOWS), POOL_GROUP_DIM, POOL_GROUP_DIM), fix3),
        pl.BlockSpec((1, POOL_WIDTH), fix2),
        pl.BlockSpec((1, POOL_WIDTH), fix2),
        pl.BlockSpec((1, ATTN_WIDTH), fix2),
        pl.BlockSpec((D_MODEL, D_MODEL), fix2),
    ]


def _merge_prompt(x, u, attn, w, *, batch, t_pad, tm):
    n = x.shape[0]
    nt = t_pad // tm
    row = lambda b, i: (b * nt + i, 0)
    return pl.pallas_call(
        _merge_prompt_kernel,
        out_shape=jax.ShapeDtypeStruct((n, D_MODEL), F32),
        grid=(batch, nt),
        in_specs=[
            pl.BlockSpec((tm, D_MODEL), row),
            pl.BlockSpec((tm, POOL_WIDTH), row),
            pl.BlockSpec((HALO, POOL_WIDTH),
                         lambda b, i: (jnp.maximum((b * nt + i) * (tm // HALO) - 1, 0), 0)),
            pl.BlockSpec((N_HEADS, None, V_DIM, tm), lambda b, i: (0, b * nt + i, 0, 0)),
        ] + _merge_weight_specs(lambda b, i: (0, 0), lambda b, i: (0, 0, 0)),
        out_specs=pl.BlockSpec((tm, D_MODEL), row),
        scratch_shapes=[pltpu.VMEM((HALO + tm, POOL_WIDTH), F32)],
        compiler_params=_params(("parallel", "arbitrary")),
        name="merge_prompt",
    )(x, u, u, attn, w["pool_w"], w["pool_scale"], w["pool_out_norm"], w["attn_out_norm"], w["w_out"])


def _merge_sample_kernel(x_ref, st_ref, us_ref, pc_ref, wv_ref, pw_ref, ps_ref, pn_ref, an_ref, wo_ref,
                         o_ref):
    def ext_row(j, sl):
        return st_ref[j, :, sl] if j < POOL_STATE else us_ref[j - POOL_STATE, :, sl]

    rows = []
    for s in range(us_ref.shape[0]):
        ds = []
        for g, w in enumerate(POOL_WINDOWS):
            sl = slice(g * POOL_GROUP_DIM, (g + 1) * POOL_GROUP_DIM)
            tok = ext_row(POOL_STATE + s, sl)
            acc = tok
            for k in range(1, w):
                acc = acc + ext_row(POOL_STATE + s - k, sl)
            ds.append(acc * (1.0 / w) - tok)
        rows.append(jnp.concatenate(ds, axis=-1))
    d = jnp.concatenate(rows, axis=0)
    a = jnp.concatenate([_dot(pc_ref[hd].astype(BF16), wv_ref[hd]) for hd in range(N_HEADS)], axis=-1)
    o_ref[...] = _merge_tail(x_ref[...], d, a, pw_ref, ps_ref, pn_ref, an_ref, wo_ref)


def _merge_sample(x, state_t, us, pc, w, *, layer):
    n = x.shape[0]
    dec_seq, dec_batch, _ = us.shape
    once = pl.Buffered(1)
    return pl.pallas_call(
        _merge_sample_kernel,
        out_shape=jax.ShapeDtypeStruct((n, D_MODEL), F32),
        grid=(1,),
        in_specs=[
            pl.BlockSpec((n, D_MODEL), lambda s: (0, 0), pipeline_mode=once),
            pl.BlockSpec((None, POOL_STATE, dec_batch, POOL_WIDTH), lambda s: (layer, 0, 0, 0),
                         pipeline_mode=once),
            pl.BlockSpec((dec_seq, dec_batch, POOL_WIDTH), lambda s: (0, 0, 0), pipeline_mode=once),
            pl.BlockSpec((N_HEADS, n, KV_LORA), lambda s: (0, 0, 0), pipeline_mode=once),
            pl.BlockSpec((N_HEADS, KV_LORA, V_DIM), lambda s: (0, 0, 0), pipeline_mode=once),
            pl.BlockSpec((len(POOL_WINDOWS), POOL_GROUP_DIM, POOL_GROUP_DIM), lambda s: (0, 0, 0),
                         pipeline_mode=once),
            pl.BlockSpec((1, POOL_WIDTH), lambda s: (0, 0)),
            pl.BlockSpec((1, POOL_WIDTH), lambda s: (0, 0)),
            pl.BlockSpec((1, ATTN_WIDTH), lambda s: (0, 0)),
            pl.BlockSpec((D_MODEL, D_MODEL), lambda s: (0, 0), pipeline_mode=once),
        ],
        out_specs=pl.BlockSpec((n, D_MODEL), lambda s: (0, 0)),
        compiler_params=_params(("arbitrary",)),
        name="merge_sample",
    )(x, state_t, us, pc, w["wv3"], w["pool_w"], w["pool_scale"], w["pool_out_norm"],
      w["attn_out_norm"], w["w_out"])


def _qabs_kernel(q_ref, k_ref, wk_ref, gk_ref, qa_ref, qr_ref, sn_ref, *, dec_seq):
    qb = q_ref[...]
    q = qb.astype(F32)
    gk = gk_ref[...]
    qn = (q[:, :NOPE_DIM] * gk[:, :NOPE_DIM]).astype(BF16)
    qa_ref[...] = _dot_nt(qn, wk_ref[...]).astype(BF16)
    qr_ref[...] = (q[:, NOPE_DIM:] * gk[:, NOPE_DIM:]).astype(BF16)
    nb = qb.shape[0] // dec_seq
    eye = (lax.broadcasted_iota(jnp.int32, (nb, nb), 0) == lax.broadcasted_iota(jnp.int32, (nb, nb), 1))
    for s in range(dec_seq):
        for j in range(dec_seq):
            r = s * dec_seq + j
            if j <= s:
                mm = _dot_nt(qb[s * nb:(s + 1) * nb], k_ref[j * nb:(j + 1) * nb, :])
                sn_ref[r:r + 1, :] = jnp.sum(jnp.where(eye, mm, 0.0), axis=0, keepdims=True)
            else:
                sn_ref[r:r + 1, :] = jnp.full((1, nb), NEG_INF, F32)


def _qabs(q, k, w, *, dec_seq):
    n = q.shape[1]
    head = lambda h: (h, 0, 0)
    return pl.pallas_call(
        functools.partial(_qabs_kernel, dec_seq=dec_seq),
        out_shape=(jax.ShapeDtypeStruct((N_HEADS, n, KV_LORA), BF16),
                   jax.ShapeDtypeStruct((N_HEADS, n, QK_PAD - NOPE_DIM), BF16),
                   jax.ShapeDtypeStruct((N_HEADS, dec_seq * dec_seq, n // dec_seq), F32)),
        grid=(N_HEADS,),
        in_specs=[
            pl.BlockSpec((None, n, QK_PAD), head),
            pl.BlockSpec((None, n, QK_PAD), head),
            pl.BlockSpec((None, KV_LORA, NOPE_DIM), head),
            pl.BlockSpec((1, QK_PAD), lambda h: (0, 0)),
        ],
        out_specs=(pl.BlockSpec((None, n, KV_LORA), head),
                   pl.BlockSpec((None, n, QK_PAD - NOPE_DIM), head),
                   pl.BlockSpec((None, dec_seq * dec_seq, n // dec_seq), head)),
        compiler_params=_params(("parallel",)),
        name="qabs",
    )(q, k, w["wk3"], w["gk"])


def _sattn_kernel(pt_ref, qa_ref, qr_ref, snew_ref, cnew_ref, wkt_ref, ckv_hbm, kpe_hbm, o_ref,
                  a_ref, cbuf, pbuf, cb16, s_all, sem, *, layer, n_pages, n_rows):
    b = pl.program_id(0)
    nb = pl.num_programs(0)
    slot = b % 2
    n_kn = N_HEADS * NOPE_DIM

    def page_copies(page, p, sl):
        return (pltpu.make_async_copy(ckv_hbm.at[layer, page], cbuf.at[sl, p], sem.at[0, sl]),
                pltpu.make_async_copy(kpe_hbm.at[layer, page], pbuf.at[sl, p], sem.at[1, sl]))

    def issue(seq, sl):
        def body(p, carry):
            for cp in page_copies(pt_ref[seq, p], p, sl):
                cp.start()
            return carry
        lax.fori_loop(0, n_pages, body, 0, unroll=True)

    def wait(sl):
        for p in range(n_pages):
            for cp in page_copies(0, p, sl):
                cp.wait()

    @pl.when(b == 0)
    def _():
        a_ref[0:n_kn, :] = wkt_ref[...]
        issue(0, 0)

    @pl.when(b + 1 < nb)
    def _():
        issue(b + 1, 1 - slot)

    a_ref[n_kn:n_kn + n_rows, :] = qa_ref[...]
    qr = qr_ref[...]
    n_chunks = n_pages * PAGE_SIZE // KEY_CHUNK
    pages_per_chunk = KEY_CHUNK // PAGE_SIZE

    def scores(cb, pt):
        nk = cb.shape[0]
        r_all = _dot_nt(a_ref[...], cb)
        kn = r_all[:n_kn]
        ssq = jnp.sum((kn * kn).reshape(N_HEADS, NOPE_DIM, nk), axis=1)
        pe2 = jnp.sum(pt * pt, axis=0, keepdims=True)
        r = lax.rsqrt((ssq + pe2) * (1.0 / QK_DIM) + EPS)
        raw_r = _dot(qr, pt.astype(BF16))
        return (r_all[n_kn:] + raw_r) * jnp.concatenate([r] * (n_rows // N_HEADS), axis=0)

    def score_group(g):
        for j in range(CHUNK_GROUP):
            c = g * CHUNK_GROUP + j
            cb = cbuf[slot, pl.ds(c * pages_per_chunk, pages_per_chunk)] \
                .reshape(KEY_CHUNK, KV_LORA).astype(BF16)
            cb16[pl.ds(pl.multiple_of(c * KEY_CHUNK, KEY_CHUNK), KEY_CHUNK), :] = cb
            pt = jnp.concatenate([pbuf[slot, c * pages_per_chunk + k] for k in range(pages_per_chunk)],
                                 axis=1)
            s_all[c] = scores(cb, pt)

    def accumulate(s, c_rows, carry):
        m, l, acc = carry
        m_new = jnp.maximum(m, jnp.max(s, axis=-1, keepdims=True))
        p = jnp.exp2(s - m_new)
        alpha = jnp.exp2(m - m_new)
        l = alpha * l + jnp.sum(p, axis=-1, keepdims=True)
        acc = alpha * acc + _dot(p.astype(BF16), c_rows)
        return m_new, l, acc

    def accumulate_group(g, carry):
        s = jnp.concatenate([s_all[g * CHUNK_GROUP + j] for j in range(CHUNK_GROUP)], axis=1)
        n_g = CHUNK_GROUP * KEY_CHUNK
        return accumulate(s, cb16[pl.ds(pl.multiple_of(g * n_g, n_g), n_g), :], carry)

    wait(slot)
    n_groups = n_chunks // CHUNK_GROUP
    score_group(0)

    def body(g, carry):
        carry = accumulate_group(g - 1, carry)
        score_group(g)
        return carry

    carry = (jnp.full((n_rows, 1), -jnp.inf, F32), jnp.zeros((n_rows, 1), F32),
             jnp.zeros((n_rows, KV_LORA), F32))
    carry = lax.fori_loop(1, n_groups, body, carry)
    carry = accumulate_group(n_groups - 1, carry)
    _, l, acc = accumulate(snew_ref[...], cnew_ref[...], carry)
    o_ref[...] = acc / l


def _sattn(page_table, qa, qr, snew, cnew, wkt, cache_ckv, cache_kpe_t, *, layer):
    nb, n_pages = page_table.shape
    n_rows = qa.shape[1]
    n_keys = n_pages * PAGE_SIZE
    n_chunks = n_keys // KEY_CHUNK
    assert n_chunks % CHUNK_GROUP == 0 and KEY_CHUNK % PAGE_SIZE == 0
    seq3 = lambda b, pt: (b, 0, 0)
    grid_spec = pltpu.PrefetchScalarGridSpec(
        num_scalar_prefetch=1,
        grid=(nb,),
        in_specs=[
            pl.BlockSpec((None, n_rows, KV_LORA), seq3),
            pl.BlockSpec((None, n_rows, ROPE_DIM), seq3),
            pl.BlockSpec((None, n_rows, PAGE_SIZE), seq3),
            pl.BlockSpec((None, PAGE_SIZE, KV_LORA), seq3),
            pl.BlockSpec((N_HEADS * NOPE_DIM, KV_LORA), lambda b, pt: (0, 0)),
            pl.BlockSpec(memory_space=pl.ANY),
            pl.BlockSpec(memory_space=pl.ANY),
        ],
        out_specs=pl.BlockSpec((None, n_rows, KV_LORA), seq3),
        scratch_shapes=[
            pltpu.VMEM((N_HEADS * NOPE_DIM + n_rows, KV_LORA), BF16),
            pltpu.VMEM((2, n_pages, PAGE_SIZE, KV_LORA), F32),
            pltpu.VMEM((2, n_pages, ROPE_DIM, PAGE_SIZE), F32),
            pltpu.VMEM((n_keys, KV_LORA), BF16),
            pltpu.VMEM((n_chunks, n_rows, KEY_CHUNK), F32),
            pltpu.SemaphoreType.DMA((2, 2)),
        ],
    )
    return pl.pallas_call(
        functools.partial(_sattn_kernel, layer=layer, n_pages=n_pages, n_rows=n_rows),
        out_shape=jax.ShapeDtypeStruct((nb, n_rows, KV_LORA), F32),
        grid_spec=grid_spec,
        compiler_params=_params(("arbitrary",)),
        name="sattn",
    )(page_table, qa, qr, snew, cnew, wkt, cache_ckv, cache_kpe_t)


def _rot_half_cols(w):
    return jnp.concatenate([-w[..., ROPE_HALF:], w[..., :ROPE_HALF]], axis=-1)


def _rope_table(pos):
    inv = ROPE_THETA ** (-jnp.arange(ROPE_HALF, dtype=F32) / ROPE_HALF)
    ang = pos.astype(F32)[:, None] * inv[None, :]
    c, s = jnp.cos(ang), jnp.sin(ang)
    return jnp.concatenate([c, c, s, s], axis=-1)


def _layer_weights(l, p):
    row = lambda v: v[l][None, :].astype(F32)
    wq = p["w_q_b"][l].reshape(Q_LORA, N_HEADS, QK_DIM)
    wq_rope = wq[..., NOPE_DIM:]
    wkv = p["w_kv_b"][l].reshape(KV_LORA, N_HEADS, NOPE_DIM + V_DIM)
    wk = wkv[..., :NOPE_DIM]
    wv = wkv[..., NOPE_DIM:]
    zeros = jnp.zeros((QK_PAD - QK_DIM,), F32)

    def head_gain(g_nope, g_rope):
        return jnp.concatenate([g_nope[l], g_rope[l], g_rope[l], zeros])[None, :]

    out = {
        "mix_norm": row(p["mix_norm"]),
        "q_a_norm": row(p["q_a_norm"]),
        "wq": jnp.concatenate([wq, _rot_half_cols(wq_rope)], axis=-1)
              .reshape(Q_LORA, N_HEADS * QK_PAD).astype(BF16),
        "kv_a_norm": row(p["kv_a_norm"]),
        "wk": wk.reshape(KV_LORA, N_HEADS * NOPE_DIM).astype(BF16),
        "wvt": wv.reshape(KV_LORA, N_HEADS * V_DIM).T.astype(BF16),
        "wk3": wk.transpose(1, 0, 2).astype(BF16),
        "wkt": wk.reshape(KV_LORA, N_HEADS * NOPE_DIM).T.astype(BF16),
        "wv3": wv.transpose(1, 0, 2).astype(BF16),
        "gq": head_gain(p["q_norm_nope"], p["q_norm_rope"]) * (ATTN_SCALE * LOG2_E),
        "gk": head_gain(p["k_norm_nope"], p["k_norm_rope"]),
        "pool_w": p["pool_w"][l].astype(BF16),
        "pool_scale": row(p["pool_scale"]),
        "pool_out_norm": row(p["pool_out_norm"]),
        "attn_out_norm": row(p["attn_out_norm"]),
        "w_out": p["w_out"][l].astype(BF16),
        "ffn1_norm": row(p["ffn1_norm"]),
        "ffn2_norm": row(p["ffn2_norm"]),
    }
    return out


def kernel(x_prompt, x_sample, cache_ckv, cache_kpe, state_pool, page_table, meta_tokens, ffn1_norm, ffn1_w_gate, ffn1_w_up, ffn1_w_down, mix_norm, w_in, pool_w, pool_scale, q_a_norm, w_q_b, kv_a_norm, w_kv_b, q_norm_nope, q_norm_rope, k_norm_nope, k_norm_rope, pool_out_norm, attn_out_norm, w_out, ffn2_norm, ffn2_w_gate, ffn2_w_up, ffn2_w_down):
    p = dict(ffn1_norm=ffn1_norm, ffn1_w_gate=ffn1_w_gate, ffn1_w_up=ffn1_w_up, ffn1_w_down=ffn1_w_down,
             mix_norm=mix_norm, w_in=w_in, pool_w=pool_w, pool_scale=pool_scale, q_a_norm=q_a_norm,
             w_q_b=w_q_b, kv_a_norm=kv_a_norm, w_kv_b=w_kv_b, q_norm_nope=q_norm_nope,
             q_norm_rope=q_norm_rope, k_norm_nope=k_norm_nope, k_norm_rope=k_norm_rope,
             pool_out_norm=pool_out_norm, attn_out_norm=attn_out_norm, w_out=w_out,
             ffn2_norm=ffn2_norm, ffn2_w_gate=ffn2_w_gate, ffn2_w_up=ffn2_w_up, ffn2_w_down=ffn2_w_down)
    depth = w_in.shape[0]
    batch, seq, _ = x_prompt.shape
    dec_batch, dec_seq, _ = x_sample.shape
    n_pages = page_table.shape[1]
    t_real = N_META + seq
    t_pad = -(-t_real // ATT_BLOCK) * ATT_BLOCK
    n_p = batch * t_pad
    n_s = dec_seq * dec_batch
    assert n_p % FFN_ROWS == 0 and dec_batch % 8 == 0 and dec_seq * N_HEADS % 8 == 0
    cache_kpe_t = jnp.swapaxes(cache_kpe, 2, 3)
    state_t = jnp.swapaxes(state_pool.astype(F32), 1, 2)
    w_in_t = jnp.swapaxes(w_in, 1, 2).astype(BF16)
    w_pe_t = w_in_t[:, POOL_WIDTH + Q_LORA + KV_LORA:]
    w_in_t = jnp.concatenate([w_in_t, -w_pe_t[:, ROPE_HALF:], w_pe_t[:, :ROPE_HALF]], axis=1)

    meta = jnp.broadcast_to(meta_tokens[None].astype(F32), (batch, N_META, D_MODEL))
    xp = jnp.concatenate([meta, x_prompt, jnp.zeros((batch, t_pad - t_real, D_MODEL), F32)], axis=1)
    xp = xp.reshape(n_p, D_MODEL)
    xs = x_sample.transpose(1, 0, 2).reshape(n_s, D_MODEL)
    cs_p = jnp.tile(_rope_table(jnp.arange(t_pad)), (batch, 1))
    cs_s = jnp.repeat(_rope_table(n_pages * PAGE_SIZE + jnp.arange(dec_seq)), dec_batch, axis=0)

    outs = [[] for _ in range(6)]
    for l in range(depth):
        w = _layer_weights(l, p)
        xs, *w16 = _ffn_cast(xs, w["ffn1_norm"], ffn1_w_gate, ffn1_w_up, ffn1_w_down, layer=l)
        xp = _ffn(xp, w["ffn1_norm"], *w16, tm=FFN_ROWS)

        u_p, ckv_p, kpe_p, q_p, k_p, v_p = _inproj(xp, cs_p, w_in_t, w, layer=l)
        attn_p = _flash(q_p, k_p, v_p, batch=batch, t_pad=t_pad)
        xp = _merge_prompt(xp, u_p, attn_p, w, batch=batch, t_pad=t_pad, tm=ATT_BLOCK)
        outs[0].append(ckv_p.reshape(batch, t_pad, KV_LORA)[:, :t_real])
        outs[1].append(kpe_p.reshape(batch, t_pad, 2 * ROPE_DIM)[:, :t_real, :ROPE_DIM])
        outs[2].append(u_p.reshape(batch, t_pad, POOL_WIDTH)[:, t_real - POOL_STATE:t_real])

        u_s, ckv_s, kpe_s, q_s, k_s, _ = _inproj(xs, cs_s, w_in_t, w, layer=l)
        qa, qr, sn = _qabs(q_s, k_s, w, dec_seq=dec_seq)

        def per_seq(t):
            return t.reshape(N_HEADS, dec_seq, dec_batch, -1).transpose(2, 1, 0, 3) \
                    .reshape(dec_batch, dec_seq * N_HEADS, -1)

        c_new = ckv_s.reshape(dec_seq, dec_batch, KV_LORA).transpose(1, 0, 2)
        p_new = kpe_s[:, :ROPE_DIM].reshape(dec_seq, dec_batch, ROPE_DIM).transpose(1, 0, 2)
        n_fill = PAGE_SIZE - dec_seq
        sn = sn.reshape(N_HEADS, dec_seq, dec_seq, dec_batch).transpose(3, 1, 0, 2) \
               .reshape(dec_batch, dec_seq * N_HEADS, dec_seq)
        pc = _sattn(page_table, per_seq(qa), per_seq(qr[..., :ROPE_DIM]),
                    jnp.pad(sn, ((0, 0), (0, 0), (0, n_fill)), constant_values=NEG_INF),
                    jnp.pad(c_new, ((0, 0), (0, n_fill), (0, 0))).astype(BF16),
                    w["wkt"], cache_ckv, cache_kpe_t, layer=l)
        pc = pc.reshape(dec_batch, dec_seq, N_HEADS, KV_LORA).transpose(2, 1, 0, 3) \
               .reshape(N_HEADS, n_s, KV_LORA)
        us = u_s.reshape(dec_seq, dec_batch, POOL_WIDTH)
        xs = _merge_sample(xs, state_t, us, pc, w, layer=l)
        outs[3].append(c_new)
        outs[4].append(p_new)
        outs[5].append(jnp.concatenate([state_t[l, dec_seq:], us], axis=0).transpose(1, 0, 2))

        xs, *w16 = _ffn_cast(xs, w["ffn2_norm"], ffn2_w_gate, ffn2_w_up, ffn2_w_down, layer=l)
        xp = _ffn(xp, w["ffn2_norm"], *w16, tm=FFN_ROWS)

    y_prompt = xp.reshape(batch, t_pad, D_MODEL)[:, N_META:t_real]
    y_sample = xs.reshape(dec_seq, dec_batch, D_MODEL).transpose(1, 0, 2)
    return (y_prompt, y_sample) + tuple(jnp.stack(o) for o in outs)
```
